```python
import jax, jax.numpy as jnp
from jax import lax
import numpy as np

D_MODEL = 1024
BATCH = 32
SEQ = 2048
DEPTH = 2
DEC_BATCH = 8
DEC_SEQ = 16
PAST_LEN = 1024

CHUNK = 64
N_EVEN = (DEPTH + 1) // 2
N_ODD = DEPTH // 2
POOL_WINDOWS = (2, 4, 8, 16)
N_POOL_GROUPS = len(POOL_WINDOWS)
D_POOL = D_MODEL // 2
POOL_GROUP = D_POOL // N_POOL_GROUPS
POOL_HIST = max(POOL_WINDOWS) - 1
SG_CHUNK = 128
D_SG = D_MODEL // 2
SG_HEADS = 4
SG_HEAD = D_SG // SG_HEADS
D_IN_EVEN = D_POOL + 2 * D_SG
GLA_HEADS = 4
GLA_DK = D_MODEL // 2 // GLA_HEADS
GLA_DV = D_MODEL // GLA_HEADS
GLA_GATE_RANK = 16
GLA_TAU = 16.0
D_QK = GLA_HEADS * GLA_DK
D_V = GLA_HEADS * GLA_DV
D_IN_ODD = 2 * D_QK + 2 * D_V + GLA_GATE_RANK
D_FF = -(-8 * D_MODEL // (3 * 256)) * 256
EPS = 1e-6

kernel_name = "streaming_pool_sgmlp_gla_hybrid_step"


def rms_norm(x, g):
    xf = x.astype(jnp.float32)
    y = xf * lax.rsqrt(jnp.mean(xf * xf, axis=-1, keepdims=True) + EPS)
    return (y * g.astype(jnp.float32)).astype(x.dtype)


def layer_norm(x, g, b):
    xf = x.astype(jnp.float32)
    mu = jnp.mean(xf, axis=-1, keepdims=True)
    var = jnp.mean(jnp.square(xf - mu), axis=-1, keepdims=True)
    y = (xf - mu) * lax.rsqrt(var + EPS) * g.astype(jnp.float32) + b.astype(jnp.float32)
    return y.astype(x.dtype)


def pool_mixer(a, hist, start, w_pool, pool_scale):
    B, L, _ = a.shape
    full = jnp.concatenate([hist.astype(a.dtype), a], axis=1)
    cs = jnp.cumsum(full.astype(jnp.float32), axis=1)
    cs = jnp.concatenate([jnp.zeros_like(cs[:, :1]), cs], axis=1)
    pos = start + jnp.arange(L)
    hi = cs[:, POOL_HIST + 1:]
    outs = []
    for gi, w in enumerate(POOL_WINDOWS):
        sl = slice(gi * POOL_GROUP, (gi + 1) * POOL_GROUP)
        lo = cs[:, POOL_HIST + 1 - w:POOL_HIST + 1 - w + L, sl]
        cnt = jnp.minimum(w, pos + 1).astype(jnp.float32)
        outs.append((hi[..., sl] - lo) / cnt[None, :, None])
    pooled = jnp.concatenate(outs, axis=-1) - a.astype(jnp.float32)
    pooled = pooled.reshape(B, L, N_POOL_GROUPS, POOL_GROUP)
    y = jnp.einsum('blgc,gcd->blgd', pooled, w_pool.astype(jnp.float32)).reshape(B, L, D_POOL)
    y = y * pool_scale.astype(jnp.float32)
    return y.astype(a.dtype), full[:, -POOL_HIST:]


def spatial_gate(u, v, w_s, b_s):
    B, L, _ = u.shape
    n = max(L // SG_CHUNK, 1)
    c = L // n
    mask = jnp.tril(jnp.ones((c, c), dtype=bool))
    w = jnp.where(mask[None], w_s[:, :c, :c], 0.0)
    vh = v.reshape(B, n, c, SG_HEADS, SG_HEAD)
    mixed = jnp.einsum('hts,bnshd->bnthd', w, vh)
    mixed = mixed + jnp.transpose(b_s[:, :c])[None, None, :, :, None]
    return u * mixed.reshape(B, L, D_SG)


def even_mixer(h, hist, start, w_in, w_pool, pool_scale, ln_v_g, ln_v_b, w_s, b_s, w_out):
    z = h @ w_in
    a = z[..., :D_POOL]
    u = jax.nn.gelu(z[..., D_POOL:D_POOL + D_SG])
    v = layer_norm(jax.nn.gelu(z[..., D_POOL + D_SG:]), ln_v_g, ln_v_b)
    ya, new_hist = pool_mixer(a, hist, start, w_pool, pool_scale)
    yb = spatial_gate(u, v, w_s, b_s)
    y = jnp.concatenate([ya, yb], axis=-1) @ w_out
    return y, new_hist, v


def gla_chunked(q, k, v, log_a, s0):
    B, L, H, _ = q.shape
    c = min(CHUNK, L)
    n = L // c

    def to_blocks(t):
        return t.astype(jnp.float32).reshape(B, n, c, H, -1).transpose(1, 0, 3, 2, 4)

    qc, kc, vc, gc = to_blocks(q), to_blocks(k), to_blocks(v), to_blocks(log_a)
    cum = jnp.cumsum(gc, axis=3)
    mask = jnp.tril(jnp.ones((c, c), dtype=bool))

    def step(s, inp):
        qi, ki, vi, bi = inp
        b_last = bi[:, :, -1:, :]
        q_dec = qi * jnp.exp(bi)
        k_inv = ki * jnp.exp(-bi)
        k_to_end = ki * jnp.exp(b_last - bi)
        att = jnp.where(mask, jnp.einsum('bhik,bhjk->bhij', q_dec, k_inv), 0.0)
        o = jnp.einsum('bhik,bhkv->bhiv', q_dec, s) + jnp.einsum('bhij,bhjv->bhiv', att, vi)
        s_new = jnp.exp(b_last[:, :, 0, :])[..., None] * s + jnp.einsum('bhjk,bhjv->bhkv', k_to_end, vi)
        return s_new, o

    s_fin, o = lax.scan(step, s0.astype(jnp.float32), (qc, kc, vc, cum))
    o = o.transpose(1, 0, 3, 2, 4).reshape(B, L, H, GLA_DV)
    return o.astype(q.dtype), s_fin.astype(s0.dtype)


def gla_mixer(h, s0, w_in, w_gate2, b_gate, g_onorm, w_out):
    B, L, _ = h.shape
    z = h @ w_in
    q = z[..., :D_QK].reshape(B, L, GLA_HEADS, GLA_DK) * (GLA_DK ** -0.5)
    k = z[..., D_QK:2 * D_QK].reshape(B, L, GLA_HEADS, GLA_DK)
    v = z[..., 2 * D_QK:2 * D_QK + D_V].reshape(B, L, GLA_HEADS, GLA_DV)
    r = z[..., 2 * D_QK + D_V:2 * D_QK + 2 * D_V]
    g_lr = z[..., 2 * D_QK + 2 * D_V:]
    log_a = jax.nn.log_sigmoid((g_lr @ w_gate2 + b_gate).astype(jnp.float32)) / GLA_TAU
    o, s_new = gla_chunked(q, k, v, log_a.reshape(B, L, GLA_HEADS, GLA_DK), s0)
    o = rms_norm(o, g_onorm.reshape(GLA_HEADS, GLA_DV)).reshape(B, L, D_V)
    y = (o * jax.nn.silu(r)) @ w_out
    return y, s_new


def swiglu(h, w_gate, w_up, w_down):
    return (jax.nn.silu(h @ w_gate) * (h @ w_up)) @ w_down


def trunk(x, start, pool_hist, gla_state, pre_mix_g, post_mix_g, pre_ffn_g, post_ffn_g,
          w_in_even, w_pool, pool_scale, ln_v_g, ln_v_b, w_s, b_s, w_out_even,
          w_in_odd, w_gate2, b_gate, g_onorm, w_out_odd, w_ffn_gate, w_ffn_up, w_ffn_down):
    new_pool, new_v, new_gla = [], [], []
    for l in range(DEPTH):
        h = rms_norm(x, pre_mix_g[l])
        if l % 2 == 0:
            e = l // 2
            y, hist, v = even_mixer(h, pool_hist[e], start, w_in_even[e], w_pool[e], pool_scale[e],
                                    ln_v_g[e], ln_v_b[e], w_s[e], b_s[e], w_out_even[e])
            new_pool.append(hist)
            new_v.append(v)
        else:
            o = l // 2
            y, s = gla_mixer(h, gla_state[o], w_in_odd[o], w_gate2[o], b_gate[o], g_onorm[o], w_out_odd[o])
            new_gla.append(s)
        x = x + rms_norm(y, post_mix_g[l])
        h = rms_norm(x, pre_ffn_g[l])
        x = x + rms_norm(swiglu(h, w_ffn_gate[l], w_ffn_up[l], w_ffn_down[l]), post_ffn_g[l])
    return x, jnp.stack(new_pool), jnp.stack(new_v), jnp.stack(new_gla)


def setup_inputs(seed: int = 0) -> dict:
    key = jax.random.key(seed)
    ks = jax.random.split(key, 32)
    nrm = lambda k, shape, s: jax.random.normal(k, shape, jnp.float32) * s
    gain = lambda k, shape: 1.0 + 0.05 * jax.random.normal(k, shape, jnp.float32)
    return {
        'x_prompt': nrm(ks[0], (BATCH, SEQ, D_MODEL), 1.0),
        'x_sample': nrm(ks[1], (DEC_BATCH, DEC_SEQ, D_MODEL), 1.0),
        'state_pool': nrm(ks[2], (N_EVEN, DEC_BATCH, POOL_HIST, D_POOL), 1.0),
        'state_gla': nrm(ks[3], (N_ODD, DEC_BATCH, GLA_HEADS, GLA_DK, GLA_DV), 1.0),
        'pre_mix_g': gain(ks[4], (DEPTH, D_MODEL)),
        'post_mix_g': gain(ks[5], (DEPTH, D_MODEL)),
        'pre_ffn_g': gain(ks[6], (DEPTH, D_MODEL)),
        'post_ffn_g': gain(ks[7], (DEPTH, D_MODEL)),
        'w_in_even': nrm(ks[8], (N_EVEN, D_MODEL, D_IN_EVEN), D_MODEL ** -0.5),
        'w_pool': nrm(ks[9], (N_EVEN, N_POOL_GROUPS, POOL_GROUP, POOL_GROUP), POOL_GROUP ** -0.5),
        'pool_scale': 1.0 + 0.1 * jax.random.normal(ks[10], (N_EVEN, D_POOL), jnp.float32),
        'ln_v_g': gain(ks[11], (N_EVEN, D_SG)),
        'ln_v_b': nrm(ks[12], (N_EVEN, D_SG), 0.02),
        'w_s': nrm(ks[13], (N_EVEN, SG_HEADS, SG_CHUNK, SG_CHUNK), SG_CHUNK ** -0.5),
        'b_s': 1.0 + 0.1 * jax.random.normal(ks[14], (N_EVEN, SG_HEADS, SG_CHUNK), jnp.float32),
        'w_out_even': nrm(ks[15], (N_EVEN, D_POOL + D_SG, D_MODEL), (D_POOL + D_SG) ** -0.5),
        'w_in_odd': nrm(ks[16], (N_ODD, D_MODEL, D_IN_ODD), D_MODEL ** -0.5),
        'w_gate2': nrm(ks[17], (N_ODD, GLA_GATE_RANK, D_QK), GLA_GATE_RANK ** -0.5),
        'b_gate': 1.0 + 0.1 * jax.random.normal(ks[18], (N_ODD, D_QK), jnp.float32),
        'g_onorm': gain(ks[19], (N_ODD, D_V)),
        'w_out_odd': nrm(ks[20], (N_ODD, D_V, D_MODEL), D_V ** -0.5),
        'w_ffn_gate': nrm(ks[21], (DEPTH, D_MODEL, D_FF), D_MODEL ** -0.5),
        'w_ffn_up': nrm(ks[22], (DEPTH, D_MODEL, D_FF), D_MODEL ** -0.5),
        'w_ffn_down': nrm(ks[23], (DEPTH, D_FF, D_MODEL), D_FF ** -0.5),
    }


def reference(x_prompt, x_sample, state_pool, state_gla, pre_mix_g, post_mix_g, pre_ffn_g, post_ffn_g,
              w_in_even, w_pool, pool_scale, ln_v_g, ln_v_b, w_s, b_s, w_out_even,
              w_in_odd, w_gate2, b_gate, g_onorm, w_out_odd, w_ffn_gate, w_ffn_up, w_ffn_down):
    weights = (pre_mix_g, post_mix_g, pre_ffn_g, post_ffn_g,
               w_in_even, w_pool, pool_scale, ln_v_g, ln_v_b, w_s, b_s, w_out_even,
               w_in_odd, w_gate2, b_gate, g_onorm, w_out_odd, w_ffn_gate, w_ffn_up, w_ffn_down)
    pool0 = jnp.zeros((N_EVEN, x_prompt.shape[0], POOL_HIST, D_POOL), x_prompt.dtype)
    gla0 = jnp.zeros((N_ODD, x_prompt.shape[0], GLA_HEADS, GLA_DK, GLA_DV), x_prompt.dtype)
    y_prompt, pool_prompt, _, gla_prompt = trunk(x_prompt, 0, pool0, gla0, *weights)
    y_sample, pool_sample, sg_v_sample, gla_sample = trunk(x_sample, PAST_LEN, state_pool, state_gla, *weights)
    return (y_prompt, y_sample, pool_prompt, pool_sample, sg_v_sample, gla_prompt, gla_sample)
```

```python
import functools
from typing import NamedTuple

import jax
import jax.numpy as jnp
from jax import lax
from jax.experimental import pallas as pl
from jax.experimental.pallas import tpu as pltpu

D_MODEL = 1024
PAST_LEN = 1024
CHUNK = 64
POOL_WINDOWS = (2, 4, 8, 16)
D_POOL = 512
POOL_GROUP = 128
POOL_HIST = 15
HIST_ROWS = 16
SG_CHUNK = 128
D_SG = 512
SG_HEADS = 4
SG_HEAD = 128
GLA_HEADS = 4
GLA_DK = 128
GLA_DV = 256
GLA_GATE_RANK = 16
GLA_TAU = 16.0
D_QK = 512
D_V = 1024
D_FF = 2816
FF_CHUNK = 256
N_FF_CHUNKS = D_FF // FF_CHUNK
EPS = 1e-6
GLA_BLOCK = 128

F32 = jnp.float32
BF16 = jnp.bfloat16
V7X_VMEM_LIMIT_BYTES = 56 * 1024 * 1024


class Tiling(NamedTuple):
    nseg: int
    seg_len: int
    start: int
    n_batch: int
    n_tiles: int

    @property
    def rows(self):
        return self.nseg * self.seg_len


def _rms(x, g):
    ms = jnp.mean(x * x, axis=-1, keepdims=True)
    return x * lax.rsqrt(ms + EPS) * g


def _gelu(x):
    c = 0.7978845608028654
    return x * (0.5 * (1.0 + jnp.tanh(c * (x + 0.044715 * (x * x * x)))))


def _silu(x):
    return x / (1.0 + jnp.exp(-x))


def _dot(a, b):
    return jnp.dot(a, b, preferred_element_type=F32)


def _even_body(t, emit_v, x_ref, hist_ref, gpre_ref, win_ref, wpool_ref, pscale_ref, lng_ref, lnb_ref,
               ws_ref, bst_ref, wout_ref, gpost_ref, xo_ref, histo_ref, *rest):
    if emit_v:
        vo_ref, abuf, ycat = rest
    else:
        abuf, ycat = rest
    s = pl.program_id(1)
    L = t.seg_len

    x = x_ref[...]
    h = _rms(x, gpre_ref[...]).astype(BF16)
    z = _dot(h, win_ref[...])
    a = z[:, :D_POOL]
    u = _gelu(z[:, D_POOL:D_POOL + D_SG])
    gv = _gelu(z[:, D_POOL + D_SG:])
    mu = jnp.mean(gv, axis=-1, keepdims=True)
    dv = gv - mu
    var = jnp.mean(dv * dv, axis=-1, keepdims=True)
    vv = dv * lax.rsqrt(var + EPS) * lng_ref[...] + lnb_ref[...]
    if emit_v:
        vo_ref[...] = vv

    @pl.when(s == 0)
    def _():
        abuf[:, 0:HIST_ROWS, :] = hist_ref[...]

    pos = t.start + s * L + lax.broadcasted_iota(jnp.int32, (L, 1), 0)
    for seg in range(t.nseg):
        r0 = seg * L
        abuf[seg, HIST_ROWS:HIST_ROWS + L, :] = a[r0:r0 + L]
        for gi, w in enumerate(POOL_WINDOWS):
            cols = slice(gi * POOL_GROUP, (gi + 1) * POOL_GROUP)
            acc = a[r0:r0 + L, cols]
            for kk in range(1, w):
                acc = acc + abuf[seg, HIST_ROWS - kk:HIST_ROWS - kk + L, cols]
            cnt = jnp.minimum(w, pos + 1).astype(F32)
            pooled = acc / cnt - a[r0:r0 + L, cols]
            ya = _dot(pooled.astype(BF16), wpool_ref[gi]) * pscale_ref[:, cols]
            ycat[r0:r0 + L, cols] = ya.astype(BF16)
        new_hist = abuf[seg, L:L + HIST_ROWS, :]
        abuf[seg, 0:HIST_ROWS, :] = new_hist
        histo_ref[seg] = new_hist

    c = min(SG_CHUNK, L)
    tril = (lax.broadcasted_iota(jnp.int32, (c, c), 0) >= lax.broadcasted_iota(jnp.int32, (c, c), 1))
    for hd in range(SG_HEADS):
        cols = slice(hd * SG_HEAD, (hd + 1) * SG_HEAD)
        wm = jnp.where(tril, ws_ref[hd, :c, :c], 0.0).astype(BF16)
        bias = bst_ref[:c, hd:hd + 1]
        for r0 in range(0, t.rows, c):
            mixed = _dot(wm, vv[r0:r0 + c, cols].astype(BF16)) + bias
            yb = u[r0:r0 + c, cols] * mixed
            ycat[r0:r0 + c, D_POOL + hd * SG_HEAD:D_POOL + (hd + 1) * SG_HEAD] = yb.astype(BF16)

    y = _dot(ycat[...], wout_ref[...])
    xo_ref[...] = x + _rms(y, gpost_ref[...])


def _gla_body(t, x_ref, st_ref, gpre_ref, wq_ref, wk_ref, wv_ref, wr_ref, wg1_ref, wg2_ref, bg_ref, gon_ref,
              wout_ref, gpost_ref, xo_ref, sto_ref, S, og):
    s = pl.program_id(1)
    L = t.seg_len
    R = t.rows
    c = min(CHUNK, L)
    nb = GLA_BLOCK
    cpb = nb // c

    @pl.when(s == 0)
    def _():
        S[...] = st_ref[...]

    x = x_ref[...]
    h = _rms(x, gpre_ref[...]).astype(BF16)
    q = _dot(h, wq_ref[...]) * (GLA_DK ** -0.5)
    k = _dot(h, wk_ref[...])
    v = _dot(h, wv_ref[...]).astype(BF16)
    r = _dot(h, wr_ref[...])
    glr = _dot(h, wg1_ref[...])
    gate = _dot(glr.astype(BF16), wg2_ref[...]) + bg_ref[...]
    la = (jnp.minimum(gate, 0.0) - jnp.log1p(jnp.exp(-jnp.abs(gate)))) / GLA_TAU

    ri = lax.broadcasted_iota(jnp.int32, (nb, nb), 0)
    ci = lax.broadcasted_iota(jnp.int32, (nb, nb), 1)
    same_chunk = (ri // c) == (ci // c)
    causal = same_chunk & (ri >= ci)
    tri = jnp.where(causal, 1.0, 0.0).astype(BF16)
    la_hi = la.astype(BF16)
    rem = la - la_hi.astype(F32)
    la_mid = rem.astype(BF16)
    la_lo = (rem - la_mid.astype(F32)).astype(BF16)

    for blk in range(R // nb):
        rb = blk * nb
        rows = slice(rb, rb + nb)
        b = _dot(tri, la_hi[rows]) + _dot(tri, la_mid[rows]) + _dot(tri, la_lo[rows])
        for hd in range(GLA_HEADS):
            kc = slice(hd * GLA_DK, (hd + 1) * GLA_DK)
            vc = slice(hd * GLA_DV, (hd + 1) * GLA_DV)
            bh = b[:, kc]
            bT = bh.T
            kT = k[rows, kc].T
            qd = (q[rows, kc] * jnp.exp(bh)).astype(BF16)
            kiT = (kT * jnp.exp(-bT)).astype(BF16)
            att = jnp.where(causal, _dot(qd, kiT), 0.0).astype(BF16)
            lhs = [att]
            decs = []
            for j in range(cpb):
                bl = bT[:, (j + 1) * c - 1:(j + 1) * c]
                in_chunk = (ci // c) == j
                lhs.append(jnp.where(in_chunk, kT * jnp.exp(bl - bT), 0.0).astype(BF16))
                decs.append(jnp.exp(bl))
            vh = v[rows, vc]
            res = _dot(jnp.concatenate(lhs, axis=0), vh)
            for j in range(cpb):
                r0 = rb + j * c
                seg = r0 // L
                Sh = S[seg, hd]
                o = _dot(qd[j * c:(j + 1) * c], Sh.astype(BF16)) + res[j * c:(j + 1) * c]
                S[seg, hd] = decs[j] * Sh + res[(1 + j) * nb:(2 + j) * nb]
                on = o * lax.rsqrt(jnp.mean(o * o, axis=-1, keepdims=True) + EPS) * gon_ref[:, vc]
                og[r0:r0 + c, vc] = (on * _silu(r[r0:r0 + c, vc])).astype(BF16)

    sto_ref[...] = S[...]
    y = _dot(og[...], wout_ref[...])
    xo_ref[...] = x + _rms(y, gpost_ref[...])


def _ffn_body(x_ref, gpre_ref, wgu_ref, wd_ref, gpost_ref, xo_ref, act):
    x = x_ref[...]
    h = _rms(x, gpre_ref[...]).astype(BF16)
    for j in range(N_FF_CHUNKS):
        gu = _dot(h, wgu_ref[j])
        act[:, j * FF_CHUNK:(j + 1) * FF_CHUNK] = (_silu(gu[:, :FF_CHUNK]) * gu[:, FF_CHUNK:]).astype(BF16)
    y = _dot(act[...], wd_ref[...])
    xo_ref[...] = x + _rms(y, gpost_ref[...])


def _const_spec(shape):
    nd = len(shape)
    return pl.BlockSpec(shape, lambda b, s: (0,) * nd, pipeline_mode=pl.Buffered(1))


def _row_spec(t, width):
    return pl.BlockSpec((t.rows, width), lambda b, s: (b * t.n_tiles + s, 0))


def _params():
    return pltpu.CompilerParams(dimension_semantics=("arbitrary", "arbitrary"),
                                vmem_limit_bytes=V7X_VMEM_LIMIT_BYTES)


def _even_call(t, emit_v, x, hist, w):
    n_rows = x.shape[0]
    consts = (w["gpre"], w["win"], w["wpool"], w["pscale"], w["lng"], w["lnb"], w["ws"], w["bst"], w["wout"],
              w["gpost"])
    out_shape = [jax.ShapeDtypeStruct((n_rows, D_MODEL), F32),
                 jax.ShapeDtypeStruct(hist.shape, F32)]
    out_specs = [_row_spec(t, D_MODEL),
                 pl.BlockSpec((t.nseg, HIST_ROWS, D_POOL), lambda b, s: (b, 0, 0))]
    if emit_v:
        out_shape.append(jax.ShapeDtypeStruct((n_rows, D_SG), F32))
        out_specs.append(_row_spec(t, D_SG))
    return pl.pallas_call(
        functools.partial(_even_body, t, emit_v),
        grid=(t.n_batch, t.n_tiles),
        in_specs=[_row_spec(t, D_MODEL),
                  pl.BlockSpec((t.nseg, HIST_ROWS, D_POOL), lambda b, s: (b, 0, 0))]
                 + [_const_spec(c.shape) for c in consts],
        out_specs=out_specs,
        out_shape=out_shape,
        scratch_shapes=[pltpu.VMEM((t.nseg, HIST_ROWS + t.seg_len, D_POOL), F32),
                        pltpu.VMEM((t.rows, D_POOL + D_SG), BF16)],
        compiler_params=_params(),
        name="even_mixer_%d" % t.seg_len,
    )(x, hist, *consts)


def _gla_call(t, x, state, w):
    n_rows = x.shape[0]
    consts = (w["gpre"], w["wq"], w["wk"], w["wv"], w["wr"], w["wg1"], w["wg2"], w["bg"], w["gon"], w["wout"],
              w["gpost"])
    st_spec = pl.BlockSpec((t.nseg, GLA_HEADS, GLA_DK, GLA_DV), lambda b, s: (b, 0, 0, 0))
    return pl.pallas_call(
        functools.partial(_gla_body, t),
        grid=(t.n_batch, t.n_tiles),
        in_specs=[_row_spec(t, D_MODEL), st_spec] + [_const_spec(c.shape) for c in consts],
        out_specs=[_row_spec(t, D_MODEL), st_spec],
        out_shape=[jax.ShapeDtypeStruct((n_rows, D_MODEL), F32),
                   jax.ShapeDtypeStruct(state.shape, F32)],
        scratch_shapes=[pltpu.VMEM((t.nseg, GLA_HEADS, GLA_DK, GLA_DV), F32),
                        pltpu.VMEM((t.rows, D_V), BF16)],
        compiler_params=_params(),
        name="gla_mixer_%d" % t.seg_len,
    )(x, state, *consts)


def _ffn_call(t, x, w):
    n_rows = x.shape[0]
    consts = (w["gpre"], w["wgu"], w["wd"], w["gpost"])
    return pl.pallas_call(
        _ffn_body,
        grid=(t.n_batch, t.n_tiles),
        in_specs=[_row_spec(t, D_MODEL)] + [_const_spec(c.shape) for c in consts],
        out_specs=_row_spec(t, D_MODEL),
        out_shape=jax.ShapeDtypeStruct((n_rows, D_MODEL), F32),
        scratch_shapes=[pltpu.VMEM((t.rows, D_FF), BF16)],
        compiler_params=_params(),
        name="ffn_%d" % t.seg_len,
    )(x, *consts)


def _row(g):
    return g.reshape(1, -1)


def _trunk(t, emit_v, x, hist, state, we, wo, wf0, wf1):
    x, hist_out, *v_out = _even_call(t, emit_v, x, hist, we)
    x = _ffn_call(t, x, wf0)
    x, state_out = _gla_call(t, x, state, wo)
    x = _ffn_call(t, x, wf1)
    return x, hist_out, (v_out[0] if emit_v else None), state_out


def kernel(x_prompt, x_sample, state_pool, state_gla, pre_mix_g, post_mix_g, pre_ffn_g, post_ffn_g, w_in_even,
           w_pool, pool_scale, ln_v_g, ln_v_b, w_s, b_s, w_out_even, w_in_odd, w_gate2, b_gate, g_onorm,
           w_out_odd, w_ffn_gate, w_ffn_up, w_ffn_down):
    B, SEQ, _ = x_prompt.shape
    DB, DSEQ, _ = x_sample.shape

    we = dict(gpre=_row(pre_mix_g[0]), win=w_in_even[0].astype(BF16), wpool=w_pool[0].astype(BF16),
              pscale=_row(pool_scale[0]), lng=_row(ln_v_g[0]), lnb=_row(ln_v_b[0]), ws=w_s[0],
              bst=jnp.transpose(b_s[0]), wout=w_out_even[0].astype(BF16), gpost=_row(post_mix_g[0]))
    wi = w_in_odd[0]
    wo = dict(gpre=_row(pre_mix_g[1]), wq=wi[:, :D_QK].astype(BF16), wk=wi[:, D_QK:2 * D_QK].astype(BF16),
              wv=wi[:, 2 * D_QK:2 * D_QK + D_V].astype(BF16),
              wr=wi[:, 2 * D_QK + D_V:2 * D_QK + 2 * D_V].astype(BF16),
              wg1=wi[:, 2 * D_QK + 2 * D_V:].astype(BF16), wg2=w_gate2[0].astype(BF16), bg=_row(b_gate[0]),
              gon=_row(g_onorm[0]), wout=w_out_odd[0].astype(BF16), gpost=_row(post_mix_g[1]))

    def ffn_weights(l):
        wg = w_ffn_gate[l].reshape(D_MODEL, N_FF_CHUNKS, FF_CHUNK)
        wu = w_ffn_up[l].reshape(D_MODEL, N_FF_CHUNKS, FF_CHUNK)
        wgu = jnp.transpose(jnp.concatenate([wg, wu], axis=-1), (1, 0, 2)).astype(BF16)
        return dict(gpre=_row(pre_ffn_g[l]), wgu=wgu, wd=w_ffn_down[l].astype(BF16), gpost=_row(post_ffn_g[l]))

    wf0, wf1 = ffn_weights(0), ffn_weights(1)

    tp = Tiling(nseg=1, seg_len=256, start=0, n_batch=B, n_tiles=SEQ // 256)
    hist0 = jnp.zeros((B, HIST_ROWS, D_POOL), F32)
    gla0 = jnp.zeros((B, GLA_HEADS, GLA_DK, GLA_DV), F32)
    yp, hp, _, sp = _trunk(tp, False, x_prompt.reshape(B * SEQ, D_MODEL), hist0, gla0, we, wo, wf0, wf1)

    ts = Tiling(nseg=DB, seg_len=DSEQ, start=PAST_LEN, n_batch=1, n_tiles=1)
    hist_s = jnp.pad(state_pool[0], ((0, 0), (HIST_ROWS - POOL_HIST, 0), (0, 0)))
    ys, hs, vs, ss = _trunk(ts, True, x_sample.reshape(DB * DSEQ, D_MODEL), hist_s, state_gla[0], we, wo, wf0, wf1)

    drop = HIST_ROWS - POOL_HIST
    return (yp.reshape(B, SEQ, D_MODEL), ys.reshape(DB, DSEQ, D_MODEL),
            hp[None, :, drop:], hs[None, :, drop:], vs.reshape(1, DB, DSEQ, D_SG), sp[None], ss[None])
```

```python
import functools
from typing import NamedTuple

import jax
import jax.numpy as jnp
from jax import lax
from jax.experimental import pallas as pl
from jax.experimental.pallas import tpu as pltpu

D_MODEL = 1024
PAST_LEN = 1024
CHUNK = 64
POOL_WINDOWS = (2, 4, 8, 16)
D_POOL = 512
POOL_GROUP = 128
POOL_HIST = 15
HIST_ROWS = 16
SG_CHUNK = 128
D_SG = 512
SG_HEADS = 4
SG_HEAD = 128
D_IN_EVEN = D_POOL + 2 * D_SG
GLA_HEADS = 4
GLA_DK = 128
GLA_DV = 256
GLA_GATE_RANK = 16
GLA_TAU = 16.0
D_QK = 512
D_V = 1024
D_FF = 2816
FF_CHUNK = 256
N_FF_CHUNKS = D_FF // FF_CHUNK
EPS = 1e-6
BLOCK_ROWS = 128
FFN_BLOCK_ROWS = 256
PROMPT_TILE_ROWS = 512
MXU_COLS = 256

F32 = jnp.float32
BF16 = jnp.bfloat16
V7X_VMEM_LIMIT_BYTES = 56 * 1024 * 1024


class Tiling(NamedTuple):
    nseg: int
    seg_len: int
    start: int
    n_batch: int
    n_tiles: int

    @property
    def rows(self):
        return self.nseg * self.seg_len


def _rms(x, g):
    ms = jnp.mean(x * x, axis=-1, keepdims=True)
    return x * lax.rsqrt(ms + EPS) * g


def _gelu(x):
    c = 0.7978845608028654
    return x * (0.5 * (1.0 + jnp.tanh(c * (x + 0.044715 * (x * x * x)))))


def _silu(x):
    return x / (1.0 + jnp.exp(-x))


def _dot(a, b):
    return jnp.dot(a, b, preferred_element_type=F32)


def _interleave(a, b):
    if len(a) < len(b):
        a, b = b, a
    out, nxt = [], 0
    for n, f in enumerate(a):
        out.append(f)
        while nxt < len(b) and (nxt + 1) * len(a) <= (n + 1) * len(b):
            out.append(b[nxt])
            nxt += 1
    return out + b[nxt:]


def _run_pipeline(blocks, project_steps, mix_steps, out_steps):
    n = len(blocks)
    states = [None] * n
    states[0], first = project_steps(blocks[0])
    for f in first:
        f()
    for i in range(n):
        free = []
        if i + 1 < n:
            states[i + 1], free = project_steps(blocks[i + 1])
        if i > 0:
            free = _interleave(free, out_steps(blocks[i - 1], states[i - 1]))
        for f in _interleave(mix_steps(blocks[i], states[i]), free):
            f()
    for f in out_steps(blocks[n - 1], states[n - 1]):
        f()


def _out_steps(y_in, wout_ref, gpost_ref, xo_ref, rows, x):
    ys = []

    def unit(n):
        ys.append(_dot(y_in[rows, :], wout_ref[:, n * MXU_COLS:(n + 1) * MXU_COLS]))

    def finish():
        xo_ref[rows, :] = x + _rms(jnp.concatenate(ys, axis=-1), gpost_ref[...])

    return [functools.partial(unit, n) for n in range(D_MODEL // MXU_COLS)] + [finish]


def _even_body(t, emit_v, x_ref, hist_ref, gpre_ref, win_ref, wpool_ref, pscale_ref, lng_ref, lnb_ref,
               ws_ref, bst_ref, wout_ref, gpost_ref, xo_ref, histo_ref, *rest):
    if emit_v:
        vo_ref, abuf, ycat = rest
    else:
        abuf, ycat = rest
    s = pl.program_id(1)
    L = t.seg_len
    nb = min(BLOCK_ROWS, t.rows)
    c = min(L, nb)

    @pl.when(s == 0)
    def _():
        abuf[:, 0:HIST_ROWS, :] = hist_ref[...]

    tril = (lax.broadcasted_iota(jnp.int32, (c, c), 0) >= lax.broadcasted_iota(jnp.int32, (c, c), 1))
    wms = [jnp.where(tril, ws_ref[hd, :c, :c], 0.0).astype(BF16) for hd in range(SG_HEADS)]
    biases = [bst_ref[:c, hd:hd + 1] for hd in range(SG_HEADS)]
    iota_c = lax.broadcasted_iota(jnp.int32, (c, 1), 0)

    def pieces(rb):
        return [(p0 // L, p0 % L, slice(p0 - rb, p0 - rb + c)) for p0 in range(rb, rb + nb, c)]

    def project_steps(rb):
        st = {"z": {}}
        rows = slice(rb, rb + nb)

        def unit(n):
            if n == 0:
                st["x"] = x_ref[rows, :]
                st["h"] = _rms(st["x"], gpre_ref[...]).astype(BF16)
            st["z"][n] = _dot(st["h"], win_ref[:, n * MXU_COLS:(n + 1) * MXU_COLS])

        def pool_rows():
            st["a"] = a = jnp.concatenate([st["z"][0], st["z"][1]], axis=-1)
            for seg, off, pr in pieces(rb):
                abuf[seg, HIST_ROWS + off:HIST_ROWS + off + c, :] = a[pr]

        def gate_u():
            st["u"] = _gelu(jnp.concatenate([st["z"][2], st["z"][3]], axis=-1))

        def gate_v():
            gv = _gelu(jnp.concatenate([st["z"][4], st["z"][5]], axis=-1))
            mu = jnp.mean(gv, axis=-1, keepdims=True)
            dv = gv - mu
            var = jnp.mean(dv * dv, axis=-1, keepdims=True)
            st["vv"] = vv = dv * lax.rsqrt(var + EPS) * lng_ref[...] + lnb_ref[...]
            if emit_v:
                vo_ref[rows, :] = vv

        u = [functools.partial(unit, n) for n in range(D_IN_EVEN // MXU_COLS)]
        return st, [u[0], u[1], pool_rows, u[2], u[3], gate_u, u[4], u[5], gate_v]

    def mix_steps(rb, st):
        def pool(seg, off, pr, gi):
            w = POOL_WINDOWS[gi]
            cols = slice(gi * POOL_GROUP, (gi + 1) * POOL_GROUP)
            base = HIST_ROWS + off
            a = st["a"][pr, cols]
            acc = a
            for kk in range(1, w):
                acc = acc + abuf[seg, base - kk:base - kk + c, cols]
            cnt = jnp.minimum(w, t.start + s * L + off + iota_c + 1).astype(F32)
            pooled = acc / cnt - a
            ya = _dot(pooled.astype(BF16), wpool_ref[gi]) * pscale_ref[:, cols]
            ycat[rb + pr.start:rb + pr.stop, cols] = ya.astype(BF16)

        def gate(pr, hd):
            cols = slice(hd * SG_HEAD, (hd + 1) * SG_HEAD)
            mixed = _dot(wms[hd], st["vv"][pr, cols].astype(BF16)) + biases[hd]
            yb = st["u"][pr, cols] * mixed
            ycat[rb + pr.start:rb + pr.stop, D_POOL + hd * SG_HEAD:D_POOL + (hd + 1) * SG_HEAD] = yb.astype(BF16)

        steps = []
        for seg, off, pr in pieces(rb):
            steps += [functools.partial(pool, seg, off, pr, gi) for gi in range(len(POOL_WINDOWS))]
            steps += [functools.partial(gate, pr, hd) for hd in range(SG_HEADS)]
        return steps

    def out_steps(rb, st):
        return _out_steps(ycat, wout_ref, gpost_ref, xo_ref, slice(rb, rb + nb), st["x"])

    _run_pipeline(list(range(0, t.rows, nb)), project_steps, mix_steps, out_steps)

    for seg in range(t.nseg):
        new_hist = abuf[seg, L:L + HIST_ROWS, :]
        abuf[seg, 0:HIST_ROWS, :] = new_hist
        histo_ref[seg] = new_hist


def _gla_body(t, x_ref, st_ref, gpre_ref, wq_ref, wk_ref, wv_ref, wr_ref, wg1_ref, wg2_ref, bg_ref, gon_ref,
              wout_ref, gpost_ref, xo_ref, sto_ref, S, og):
    s = pl.program_id(1)
    L = t.seg_len
    nb = min(BLOCK_ROWS, t.rows)
    c = min(CHUNK, L)
    cpb = nb // c

    @pl.when(s == 0)
    def _():
        S[...] = st_ref[...]

    ri = lax.broadcasted_iota(jnp.int32, (nb, nb), 0)
    ci = lax.broadcasted_iota(jnp.int32, (nb, nb), 1)
    causal = ((ri // c) == (ci // c)) & (ri >= ci)
    tri = jnp.where(causal, 1.0, 0.0).astype(BF16)

    def project_steps(rb):
        st = {}
        rows = slice(rb, rb + nb)

        def low_rank():
            st["x"] = x_ref[rows, :]
            st["h"] = _rms(st["x"], gpre_ref[...]).astype(BF16)
            st["glr"] = _dot(st["h"], wg1_ref[...])

        def log_decay():
            gate = _dot(st["glr"].astype(BF16), wg2_ref[...]) + bg_ref[...]
            la = (jnp.minimum(gate, 0.0) - jnp.log(1.0 + jnp.exp(-jnp.abs(gate)))) * (1.0 / GLA_TAU)
            la_hi = la.astype(BF16)
            rem = la - la_hi.astype(F32)
            la_mid = rem.astype(BF16)
            st["la"] = (la_hi, la_mid, (rem - la_mid.astype(F32)).astype(BF16))

        def cum_decay():
            la_hi, la_mid, la_lo = st["la"]
            st["b"] = _dot(tri, la_hi) + _dot(tri, la_mid) + _dot(tri, la_lo)

        def unit(name, w_ref, n, post):
            def f():
                st.setdefault(name, {})[n] = post(_dot(st["h"], w_ref[:, n * MXU_COLS:(n + 1) * MXU_COLS]))
            return f

        scale_q = lambda z: z * (GLA_DK ** -0.5)
        ident = lambda z: z
        to_bf16 = lambda z: z.astype(BF16)
        return st, ([low_rank, unit("q", wq_ref, 0, scale_q), log_decay, unit("q", wq_ref, 1, scale_q), cum_decay,
                     unit("k", wk_ref, 0, ident), unit("k", wk_ref, 1, ident)]
                    + [unit("v", wv_ref, n, to_bf16) for n in range(D_V // MXU_COLS)]
                    + [unit("r", wr_ref, n, ident) for n in range(D_V // MXU_COLS)])

    def mix_steps(rb, st):
        rows = slice(rb, rb + nb)

        hv = [dict() for _ in range(GLA_HEADS)]

        def scores(hd):
            kc = slice(hd * GLA_DK, (hd + 1) * GLA_DK)
            m = hv[hd]
            half = slice((hd % 2) * GLA_DK, (hd % 2 + 1) * GLA_DK)
            bh = st["b"][:, kc]
            bT = bh.T
            kT = st["k"][hd // 2][:, half].T
            m["qd"] = qd = (st["q"][hd // 2][:, half] * jnp.exp(bh)).astype(BF16)
            kiT = (kT * jnp.exp(-bT)).astype(BF16)
            m["araw"] = _dot(qd, kiT)
            m["ke"], m["dec"] = [], []
            for j in range(cpb):
                bl = bT[:, (j + 1) * c - 1:(j + 1) * c]
                in_chunk = (ci // c) == j
                m["ke"].append(jnp.where(in_chunk, kT * jnp.exp(bl - bT), 0.0).astype(BF16))
                m["dec"].append(jnp.exp(bl))

        def values(hd):
            vc = slice(hd * GLA_DV, (hd + 1) * GLA_DV)
            m = hv[hd]
            att = jnp.where(causal, m["araw"], 0.0).astype(BF16)
            m["res"] = _dot(jnp.concatenate([att] + m["ke"], axis=0), st["v"][hd])

        def recur(j, hd):
            vc = slice(hd * GLA_DV, (hd + 1) * GLA_DV)
            m = hv[hd]
            seg = (rb + j * c) // L
            cr = slice(j * c, (j + 1) * c)
            Sh = S[seg, hd]
            o = _dot(m["qd"][cr], Sh.astype(BF16)) + m["res"][cr]
            S[seg, hd] = m["dec"][j] * Sh + m["res"][(1 + j) * nb:(2 + j) * nb]
            on = o * lax.rsqrt(jnp.mean(o * o, axis=-1, keepdims=True) + EPS) * gon_ref[:, vc]
            og[rb + j * c:rb + (j + 1) * c, vc] = (on * _silu(st["r"][hd][cr])).astype(BF16)

        heads = range(GLA_HEADS)
        return ([functools.partial(scores, hd) for hd in heads]
                + [functools.partial(values, hd) for hd in heads]
                + [functools.partial(recur, j, hd) for j in range(cpb) for hd in heads])

    def out_steps(rb, st):
        return _out_steps(og, wout_ref, gpost_ref, xo_ref, slice(rb, rb + nb), st["x"])

    _run_pipeline(list(range(0, t.rows, nb)), project_steps, mix_steps, out_steps)
    sto_ref[...] = S[...]


def _ffn_body(t, x_ref, gpre_ref, wgu_ref, wd_ref, gpost_ref, xo_ref, act):
    nb = min(FFN_BLOCK_ROWS, t.rows)
    blocks = [slice(rb, rb + nb) for rb in range(0, t.rows, nb)]
    xs, hs = {}, {}

    def norm_in(i):
        xs[i] = x_ref[blocks[i], :]
        hs[i] = _rms(xs[i], gpre_ref[...]).astype(BF16)

    def up(i):
        for j in range(N_FF_CHUNKS):
            gu = _dot(hs[i], wgu_ref[j])
            act[blocks[i], j * FF_CHUNK:(j + 1) * FF_CHUNK] = (
                _silu(gu[:, :FF_CHUNK]) * gu[:, FF_CHUNK:]).astype(BF16)
            if j == 0 and i + 1 < len(blocks):
                norm_in(i + 1)

    def down(i):
        y = _dot(act[blocks[i], :], wd_ref[...])
        xo_ref[blocks[i], :] = xs[i] + _rms(y, gpost_ref[...])

    norm_in(0)
    for i in range(len(blocks)):
        up(i)
        if i > 0:
            down(i - 1)
    down(len(blocks) - 1)


def _const_spec(shape):
    nd = len(shape)
    return pl.BlockSpec(shape, lambda b, s: (0,) * nd, pipeline_mode=pl.Buffered(1))


def _row_spec(t, width):
    return pl.BlockSpec((t.rows, width), lambda b, s: (b * t.n_tiles + s, 0))


def _params():
    return pltpu.CompilerParams(dimension_semantics=("arbitrary", "arbitrary"),
                                vmem_limit_bytes=V7X_VMEM_LIMIT_BYTES)


def _even_call(t, emit_v, x, hist, w):
    n_rows = x.shape[0]
    consts = (w["gpre"], w["win"], w["wpool"], w["pscale"], w["lng"], w["lnb"], w["ws"], w["bst"], w["wout"],
              w["gpost"])
    out_shape = [jax.ShapeDtypeStruct((n_rows, D_MODEL), F32),
                 jax.ShapeDtypeStruct(hist.shape, F32)]
    out_specs = [_row_spec(t, D_MODEL),
                 pl.BlockSpec((t.nseg, HIST_ROWS, D_POOL), lambda b, s: (b, 0, 0))]
    if emit_v:
        out_shape.append(jax.ShapeDtypeStruct((n_rows, D_SG), F32))
        out_specs.append(_row_spec(t, D_SG))
    return pl.pallas_call(
        functools.partial(_even_body, t, emit_v),
        grid=(t.n_batch, t.n_tiles),
        in_specs=[_row_spec(t, D_MODEL),
                  pl.BlockSpec((t.nseg, HIST_ROWS, D_POOL), lambda b, s: (b, 0, 0))]
                 + [_const_spec(c.shape) for c in consts],
        out_specs=out_specs,
        out_shape=out_shape,
        scratch_shapes=[pltpu.VMEM((t.nseg, HIST_ROWS + t.seg_len, D_POOL), F32),
                        pltpu.VMEM((t.rows, D_POOL + D_SG), BF16)],
        compiler_params=_params(),
        name="even_mixer_%d" % t.seg_len,
    )(x, hist, *consts)


def _gla_call(t, x, state, w):
    n_rows = x.shape[0]
    consts = (w["gpre"], w["wq"], w["wk"], w["wv"], w["wr"], w["wg1"], w["wg2"], w["bg"], w["gon"], w["wout"],
              w["gpost"])
    st_spec = pl.BlockSpec((t.nseg, GLA_HEADS, GLA_DK, GLA_DV), lambda b, s: (b, 0, 0, 0))
    return pl.pallas_call(
        functools.partial(_gla_body, t),
        grid=(t.n_batch, t.n_tiles),
        in_specs=[_row_spec(t, D_MODEL), st_spec] + [_const_spec(c.shape) for c in consts],
        out_specs=[_row_spec(t, D_MODEL), st_spec],
        out_shape=[jax.ShapeDtypeStruct((n_rows, D_MODEL), F32),
                   jax.ShapeDtypeStruct(state.shape, F32)],
        scratch_shapes=[pltpu.VMEM((t.nseg, GLA_HEADS, GLA_DK, GLA_DV), F32),
                        pltpu.VMEM((t.rows, D_V), BF16)],
        compiler_params=_params(),
        name="gla_mixer_%d" % t.seg_len,
    )(x, state, *consts)


def _ffn_call(t, x, w):
    n_rows = x.shape[0]
    consts = (w["gpre"], w["wgu"], w["wd"], w["gpost"])
    return pl.pallas_call(
        functools.partial(_ffn_body, t),
        grid=(t.n_batch, t.n_tiles),
        in_specs=[_row_spec(t, D_MODEL)] + [_const_spec(c.shape) for c in consts],
        out_specs=_row_spec(t, D_MODEL),
        out_shape=jax.ShapeDtypeStruct((n_rows, D_MODEL), F32),
        scratch_shapes=[pltpu.VMEM((t.rows, D_FF), BF16)],
        compiler_params=_params(),
        name="ffn_%d" % t.seg_len,
    )(x, *consts)


def _row(g):
    return g.reshape(1, -1)


def _trunk(t, emit_v, x, hist, state, we, wo, wf0, wf1):
    x, hist_out, *v_out = _even_call(t, emit_v, x, hist, we)
    x = _ffn_call(t, x, wf0)
    x, state_out = _gla_call(t, x, state, wo)
    x = _ffn_call(t, x, wf1)
    return x, hist_out, (v_out[0] if emit_v else None), state_out


def kernel(x_prompt, x_sample, state_pool, state_gla, pre_mix_g, post_mix_g, pre_ffn_g, post_ffn_g, w_in_even,
           w_pool, pool_scale, ln_v_g, ln_v_b, w_s, b_s, w_out_even, w_in_odd, w_gate2, b_gate, g_onorm,
           w_out_odd, w_ffn_gate, w_ffn_up, w_ffn_down):
    B, SEQ, _ = x_prompt.shape
    DB, DSEQ, _ = x_sample.shape

    we = dict(gpre=_row(pre_mix_g[0]), win=w_in_even[0].astype(BF16), wpool=w_pool[0].astype(BF16),
              pscale=_row(pool_scale[0]), lng=_row(ln_v_g[0]), lnb=_row(ln_v_b[0]), ws=w_s[0],
              bst=jnp.transpose(b_s[0]), wout=w_out_even[0].astype(BF16), gpost=_row(post_mix_g[0]))
    wi = w_in_odd[0]
    wo = dict(gpre=_row(pre_mix_g[1]), wq=wi[:, :D_QK].astype(BF16), wk=wi[:, D_QK:2 * D_QK].astype(BF16),
              wv=wi[:, 2 * D_QK:2 * D_QK + D_V].astype(BF16),
              wr=wi[:, 2 * D_QK + D_V:2 * D_QK + 2 * D_V].astype(BF16),
              wg1=wi[:, 2 * D_QK + 2 * D_V:].astype(BF16), wg2=w_gate2[0].astype(BF16), bg=_row(b_gate[0]),
              gon=_row(g_onorm[0]), wout=w_out_odd[0].astype(BF16), gpost=_row(post_mix_g[1]))

    def ffn_weights(l):
        wg = w_ffn_gate[l].reshape(D_MODEL, N_FF_CHUNKS, FF_CHUNK)
        wu = w_ffn_up[l].reshape(D_MODEL, N_FF_CHUNKS, FF_CHUNK)
        wgu = jnp.transpose(jnp.concatenate([wg, wu], axis=-1), (1, 0, 2)).astype(BF16)
        return dict(gpre=_row(pre_ffn_g[l]), wgu=wgu, wd=w_ffn_down[l].astype(BF16), gpost=_row(post_ffn_g[l]))

    wf0, wf1 = ffn_weights(0), ffn_weights(1)

    tp = Tiling(nseg=1, seg_len=PROMPT_TILE_ROWS, start=0, n_batch=B, n_tiles=SEQ // PROMPT_TILE_ROWS)
    hist0 = jnp.zeros((B, HIST_ROWS, D_POOL), F32)
    gla0 = jnp.zeros((B, GLA_HEADS, GLA_DK, GLA_DV), F32)
    yp, hp, _, sp = _trunk(tp, False, x_prompt.reshape(B * SEQ, D_MODEL), hist0, gla0, we, wo, wf0, wf1)

    ts = Tiling(nseg=DB, seg_len=DSEQ, start=PAST_LEN, n_batch=1, n_tiles=1)
    hist_s = jnp.pad(state_pool[0], ((0, 0), (HIST_ROWS - POOL_HIST, 0), (0, 0)))
    ys, hs, vs, ss = _trunk(ts, True, x_sample.reshape(DB * DSEQ, D_MODEL), hist_s, state_gla[0], we, wo, wf0, wf1)

    drop = HIST_ROWS - POOL_HIST
    return (yp.reshape(B, SEQ, D_MODEL), ys.reshape(DB, DSEQ, D_MODEL),
            hp[None, :, drop:], hs[None, :, drop:], vs.reshape(1, DB, DSEQ, D_SG), sp[None], ss[None])
```

```python
import functools
from typing import NamedTuple

import jax
import jax.numpy as jnp
from jax import lax
from jax.experimental import pallas as pl
from jax.experimental.pallas import tpu as pltpu

D_MODEL = 1024
PAST_LEN = 1024
CHUNK = 64
POOL_WINDOWS = (2, 4, 8, 16)
D_POOL = 512
POOL_GROUP = 128
POOL_HIST = 15
HIST_ROWS = 16
SG_CHUNK = 128
D_SG = 512
SG_HEADS = 4
SG_HEAD = 128
D_IN_EVEN = D_POOL + 2 * D_SG
GLA_HEADS = 4
GLA_DK = 128
GLA_DV = 256
GLA_GATE_RANK = 16
GLA_TAU = 16.0
D_QK = 512
D_V = 1024
D_FF = 2816
FF_CHUNK = 256
N_FF_CHUNKS = D_FF // FF_CHUNK
EPS = 1e-6
BLOCK_ROWS = 128
FFN_BLOCK_ROWS = 256
PROMPT_TILE_ROWS = 512
MXU_COLS = 256

F32 = jnp.float32
BF16 = jnp.bfloat16
V7X_VMEM_LIMIT_BYTES = 56 * 1024 * 1024


class Tiling(NamedTuple):
    nseg: int
    seg_len: int
    start: int
    n_batch: int
    n_tiles: int

    @property
    def rows(self):
        return self.nseg * self.seg_len

    @property
    def n_row_tiles(self):
        return self.n_batch * self.n_tiles


def _rms(x, g):
    ms = jnp.mean(x * x, axis=-1, keepdims=True)
    return x * lax.rsqrt(ms + EPS) * g


def _gelu(x):
    c = 0.7978845608028654
    return x * (0.5 * (1.0 + jnp.tanh(c * (x + 0.044715 * (x * x * x)))))


def _silu(x):
    return x / (1.0 + jnp.exp(-x))


def _dot(a, b):
    return jnp.dot(a, b, preferred_element_type=F32)


def _interleave(a, b):
    if len(a) < len(b):
        a, b = b, a
    out, nxt = [], 0
    for n, f in enumerate(a):
        out.append(f)
        while nxt < len(b) and (nxt + 1) * len(a) <= (n + 1) * len(b):
            out.append(b[nxt])
            nxt += 1
    return out + b[nxt:]


def _on_parity(g, step):
    for slot in (0, 1):
        pl.when(g % 2 == slot)(functools.partial(step, slot, 1 - slot))


def _run_two_stage(blocks, project_steps, mix_steps, out_steps, n_tail):
    proj = [project_steps(rb) for rb in blocks]
    keep = len(proj[-1]) - n_tail
    tail_free, proj[-1] = proj[-1][keep:], proj[-1][:keep]
    for i, rb in enumerate(blocks):
        free = proj[i]
        if i > 0:
            free = _interleave(free, out_steps(blocks[i - 1]))
        for f in _interleave(mix_steps(rb), free):
            f()
    for f in _interleave(out_steps(blocks[-1]), tail_free):
        f()


def _out_steps(y_in, wout_ref, gpost_ref, xo_ref, rows, x_ref):
    ys = []

    def unit(n):
        ys.append(_dot(y_in[rows, :], wout_ref[:, n * MXU_COLS:(n + 1) * MXU_COLS]))

    def finish():
        xo_ref[rows, :] = x_ref[rows, :] + _rms(jnp.concatenate(ys, axis=-1), gpost_ref[...])

    return [functools.partial(unit, n) for n in range(D_MODEL // MXU_COLS)] + [finish]


def _even_body(t, emit_v, x_ref, xp_ref, hist_ref, gpre_ref, win_ref, wpool_ref, pscale_ref, lng_ref, lnb_ref,
               ws_ref, bst_ref, wout_ref, gpost_ref, xo_ref, histo_ref, *rest):
    if emit_v:
        vo_ref, abuf, us, vvs, ycat, wmask = rest
    else:
        abuf, us, vvs, ycat, wmask = rest
    g = pl.program_id(0)
    tpos = jnp.maximum(g - 1, 0) % t.n_tiles
    L = t.seg_len
    nb = min(BLOCK_ROWS, t.rows)
    c = min(L, nb)

    @pl.when(g == 0)
    def _():
        abuf[...] = jnp.zeros(abuf.shape, F32)
        us[1] = jnp.zeros(us.shape[1:], F32)
        vvs[1] = jnp.zeros(vvs.shape[1:], BF16)
        tril = (lax.broadcasted_iota(jnp.int32, (c, c), 0) >= lax.broadcasted_iota(jnp.int32, (c, c), 1))
        for hd in range(SG_HEADS):
            wmask[hd] = jnp.where(tril, ws_ref[hd, :c, :c], 0.0).astype(BF16)

    biases = [bst_ref[:c, hd:hd + 1] for hd in range(SG_HEADS)]
    iota_c = lax.broadcasted_iota(jnp.int32, (c, 1), 0)

    def pieces(rb):
        return [(p0 // L, p0 % L, slice(p0, p0 + c)) for p0 in range(rb, rb + nb, c)]

    def project_steps(w, rb):
        st = {"z": {}}
        rows = slice(rb, rb + nb)

        def unit(n):
            if n == 0:
                st["h"] = _rms(x_ref[rows, :], gpre_ref[...]).astype(BF16)
            st["z"][n] = _dot(st["h"], win_ref[:, n * MXU_COLS:(n + 1) * MXU_COLS])

        def pool_rows():
            a = jnp.concatenate([st["z"][0], st["z"][1]], axis=-1)
            for seg, off, pr in pieces(rb):
                abuf[w, seg, HIST_ROWS + off:HIST_ROWS + off + c, :] = a[pr.start - rb:pr.stop - rb]

        def gate_u():
            us[w, rows, :] = _gelu(jnp.concatenate([st["z"][2], st["z"][3]], axis=-1))

        def gate_v():
            gv = _gelu(jnp.concatenate([st["z"][4], st["z"][5]], axis=-1))
            mu = jnp.mean(gv, axis=-1, keepdims=True)
            dv = gv - mu
            var = jnp.mean(dv * dv, axis=-1, keepdims=True)
            vv = dv * lax.rsqrt(var + EPS) * lng_ref[...] + lnb_ref[...]
            vvs[w, rows, :] = vv.astype(BF16)
            if emit_v:
                vo_ref[rows, :] = vv

        u = [functools.partial(unit, n) for n in range(D_IN_EVEN // MXU_COLS)]
        return [u[0], u[1], pool_rows, u[2], u[3], gate_u, u[4], u[5], gate_v]

    def mix_steps(r, rb):
        def pool(seg, off, pr, gi):
            win = POOL_WINDOWS[gi]
            cols = slice(gi * POOL_GROUP, (gi + 1) * POOL_GROUP)
            base = HIST_ROWS + off
            a = abuf[r, seg, base:base + c, cols]
            acc = a
            for kk in range(1, win):
                acc = acc + abuf[r, seg, base - kk:base - kk + c, cols]
            cnt = jnp.minimum(win, t.start + tpos * L + off + iota_c + 1).astype(F32)
            pooled = acc / cnt - a
            ya = _dot(pooled.astype(BF16), wpool_ref[gi]) * pscale_ref[:, cols]
            ycat[pr, cols] = ya.astype(BF16)

        def gate(pr, hd):
            cols = slice(hd * SG_HEAD, (hd + 1) * SG_HEAD)
            mixed = _dot(wmask[hd], vvs[r, pr, cols]) + biases[hd]
            yb = us[r, pr, cols] * mixed
            ycat[pr, D_POOL + hd * SG_HEAD:D_POOL + (hd + 1) * SG_HEAD] = yb.astype(BF16)

        steps = []
        for seg, off, pr in pieces(rb):
            steps += [functools.partial(pool, seg, off, pr, gi) for gi in range(len(POOL_WINDOWS))]
            steps += [functools.partial(gate, pr, hd) for hd in range(SG_HEADS)]
        return steps

    def out_steps(rb):
        return _out_steps(ycat, wout_ref, gpost_ref, xo_ref, slice(rb, rb + nb), xp_ref)

    def step(w, r):
        carried = abuf[w, :, L:L + HIST_ROWS, :]
        abuf[r, :, 0:HIST_ROWS, :] = jnp.where(tpos == 0, hist_ref[...], carried)
        _run_two_stage(list(range(0, t.rows, nb)), functools.partial(project_steps, w),
                       functools.partial(mix_steps, r), out_steps, n_tail=4)
        histo_ref[...] = abuf[r, :, L:L + HIST_ROWS, :]

    _on_parity(g, step)


def _gla_body(t, x_ref, xp_ref, st_ref, gpre_ref, wq_ref, wk_ref, wv_ref, wr_ref, wg1_ref, wg2_ref, bg_ref,
              gon_ref, wout_ref, gpost_ref, xo_ref, sto_ref, S, og, qs, ks, vs, rs, bs):
    g = pl.program_id(0)
    tpos = jnp.maximum(g - 1, 0) % t.n_tiles
    L = t.seg_len
    nb = min(BLOCK_ROWS, t.rows)
    c = min(CHUNK, L)
    cpb = nb // c

    @pl.when(g == 0)
    def _():
        for ref in (qs, ks, vs, rs, bs):
            ref[1] = jnp.zeros(ref.shape[1:], ref.dtype)

    @pl.when(tpos == 0)
    def _():
        S[...] = st_ref[...]

    ri = lax.broadcasted_iota(jnp.int32, (nb, nb), 0)
    ci = lax.broadcasted_iota(jnp.int32, (nb, nb), 1)
    causal = ((ri // c) == (ci // c)) & (ri >= ci)
    tri = jnp.where(causal, 1.0, 0.0).astype(BF16)

    def project_steps(w, rb):
        st = {}
        rows = slice(rb, rb + nb)

        def low_rank():
            st["h"] = _rms(x_ref[rows, :], gpre_ref[...]).astype(BF16)
            st["glr"] = _dot(st["h"], wg1_ref[...])

        def log_decay():
            gate = _dot(st["glr"].astype(BF16), wg2_ref[...]) + bg_ref[...]
            la = (jnp.minimum(gate, 0.0) - jnp.log(1.0 + jnp.exp(-jnp.abs(gate)))) * (1.0 / GLA_TAU)
            la_hi = la.astype(BF16)
            rem = la - la_hi.astype(F32)
            la_mid = rem.astype(BF16)
            st["la"] = (la_hi, la_mid, (rem - la_mid.astype(F32)).astype(BF16))

        def cum_decay():
            la_hi, la_mid, la_lo = st["la"]
            bs[w, rows, :] = _dot(tri, la_hi) + _dot(tri, la_mid) + _dot(tri, la_lo)

        def unit(dst, w_ref, n, post):
            def f():
                cols = slice(n * MXU_COLS, (n + 1) * MXU_COLS)
                dst[w, rows, cols] = post(_dot(st["h"], w_ref[:, cols]))
            return f

        scale_q = lambda z: z * (GLA_DK ** -0.5)
        ident = lambda z: z
        to_bf16 = lambda z: z.astype(BF16)
        return ([low_rank, unit(qs, wq_ref, 0, scale_q), log_decay, unit(qs, wq_ref, 1, scale_q), cum_decay,
                 unit(ks, wk_ref, 0, ident), unit(ks, wk_ref, 1, ident)]
                + [unit(vs, wv_ref, n, to_bf16) for n in range(D_V // MXU_COLS)]
                + [unit(rs, wr_ref, n, ident) for n in range(D_V // MXU_COLS)])

    def mix_steps(r, rb):
        rows = slice(rb, rb + nb)
        hv = [dict() for _ in range(GLA_HEADS)]

        def scores(hd):
            kc = slice(hd * GLA_DK, (hd + 1) * GLA_DK)
            m = hv[hd]
            bh = bs[r, rows, kc]
            bT = bh.T
            kT = ks[r, rows, kc].T
            m["qd"] = qd = (qs[r, rows, kc] * jnp.exp(bh)).astype(BF16)
            kiT = (kT * jnp.exp(-bT)).astype(BF16)
            m["araw"] = _dot(qd, kiT)
            m["ke"], m["dec"] = [], []
            for j in range(cpb):
                bl = bT[:, (j + 1) * c - 1:(j + 1) * c]
                in_chunk = (ci // c) == j
                m["ke"].append(jnp.where(in_chunk, kT * jnp.exp(bl - bT), 0.0).astype(BF16))
                m["dec"].append(jnp.exp(bl))

        def values(hd):
            vc = slice(hd * GLA_DV, (hd + 1) * GLA_DV)
            m = hv[hd]
            att = jnp.where(causal, m["araw"], 0.0).astype(BF16)
            m["res"] = _dot(jnp.concatenate([att] + m["ke"], axis=0), vs[r, rows, vc])

        def recur(j, hd):
            vc = slice(hd * GLA_DV, (hd + 1) * GLA_DV)
            m = hv[hd]
            seg = (rb + j * c) // L
            cr = slice(j * c, (j + 1) * c)
            tr = slice(rb + j * c, rb + (j + 1) * c)
            Sh = S[seg, hd]
            o = _dot(m["qd"][cr], Sh.astype(BF16)) + m["res"][cr]
            S[seg, hd] = m["dec"][j] * Sh + m["res"][(1 + j) * nb:(2 + j) * nb]
            on = o * lax.rsqrt(jnp.mean(o * o, axis=-1, keepdims=True) + EPS) * gon_ref[:, vc]
            og[tr, vc] = (on * _silu(rs[r, tr, vc])).astype(BF16)

        heads = range(GLA_HEADS)
        return ([functools.partial(scores, hd) for hd in heads]
                + [functools.partial(values, hd) for hd in heads]
                + [functools.partial(recur, j, hd) for j in range(cpb) for hd in heads])

    def out_steps(rb):
        return _out_steps(og, wout_ref, gpost_ref, xo_ref, slice(rb, rb + nb), xp_ref)

    def step(w, r):
        _run_two_stage(list(range(0, t.rows, nb)), functools.partial(project_steps, w),
                       functools.partial(mix_steps, r), out_steps, n_tail=5)
        sto_ref[...] = S[...]

    _on_parity(g, step)


def _ffn_body(t, x_ref, xp_ref, gpre_ref, wg_ref, wu_ref, wd_ref, gpost_ref, xo_ref, act):
    g = pl.program_id(0)

    @pl.when(g == 0)
    def _():
        act[1] = jnp.zeros(act.shape[1:], BF16)

    nb = min(FFN_BLOCK_ROWS, t.rows)

    def step(w, r):
        for rb in range(0, t.rows, nb):
            rows = slice(rb, rb + nb)
            h = _rms(x_ref[rows, :], gpre_ref[...]).astype(BF16)
            y = _dot(act[r, rows, :], wd_ref[...])
            xo_ref[rows, :] = xp_ref[rows, :] + _rms(y, gpost_ref[...])
            for j in range(N_FF_CHUNKS):
                cols = slice(j * FF_CHUNK, (j + 1) * FF_CHUNK)
                act[w, rows, cols] = (_silu(_dot(h, wg_ref[:, cols])) * _dot(h, wu_ref[:, cols])).astype(BF16)

    _on_parity(g, step)


def _const_spec(shape):
    nd = len(shape)
    return pl.BlockSpec(shape, lambda g: (0,) * nd, pipeline_mode=pl.Buffered(1))


def _cur_rows(t, width):
    last = t.n_row_tiles - 1
    return pl.BlockSpec((t.rows, width), lambda g: (jnp.minimum(g, last), 0))


def _prev_rows(t, width):
    return pl.BlockSpec((t.rows, width), lambda g: (jnp.maximum(g - 1, 0), 0))


def _prev_seq(t, shape):
    nd = len(shape)
    return pl.BlockSpec((t.nseg,) + shape, lambda g: (jnp.maximum(g - 1, 0) // t.n_tiles,) + (0,) * nd)


def _params():
    return pltpu.CompilerParams(dimension_semantics=("arbitrary",), vmem_limit_bytes=V7X_VMEM_LIMIT_BYTES)


def _even_call(t, emit_v, x, hist, w):
    n_rows = x.shape[0]
    consts = (w["gpre"], w["win"], w["wpool"], w["pscale"], w["lng"], w["lnb"], w["ws"], w["bst"], w["wout"],
              w["gpost"])
    out_shape = [jax.ShapeDtypeStruct((n_rows, D_MODEL), F32),
                 jax.ShapeDtypeStruct(hist.shape, F32)]
    out_specs = [_prev_rows(t, D_MODEL), _prev_seq(t, (HIST_ROWS, D_POOL))]
    if emit_v:
        out_shape.append(jax.ShapeDtypeStruct((n_rows, D_SG), F32))
        out_specs.append(_cur_rows(t, D_SG))
    return pl.pallas_call(
        functools.partial(_even_body, t, emit_v),
        grid=(t.n_row_tiles + 1,),
        in_specs=[_cur_rows(t, D_MODEL), _prev_rows(t, D_MODEL), _prev_seq(t, (HIST_ROWS, D_POOL))]
                 + [_const_spec(c.shape) for c in consts],
        out_specs=out_specs,
        out_shape=out_shape,
        scratch_shapes=[pltpu.VMEM((2, t.nseg, HIST_ROWS + t.seg_len, D_POOL), F32),
                        pltpu.VMEM((2, t.rows, D_SG), F32),
                        pltpu.VMEM((2, t.rows, D_SG), BF16),
                        pltpu.VMEM((t.rows, D_POOL + D_SG), BF16),
                        pltpu.VMEM((SG_HEADS, min(t.seg_len, BLOCK_ROWS), min(t.seg_len, BLOCK_ROWS)), BF16)],
        compiler_params=_params(),
        name="even_mixer_%d" % t.seg_len,
    )(x, x, hist, *consts)


def _gla_call(t, x, state, w):
    n_rows = x.shape[0]
    consts = (w["gpre"], w["wq"], w["wk"], w["wv"], w["wr"], w["wg1"], w["wg2"], w["bg"], w["gon"], w["wout"],
              w["gpost"])
    st_spec = _prev_seq(t, (GLA_HEADS, GLA_DK, GLA_DV))
    return pl.pallas_call(
        functools.partial(_gla_body, t),
        grid=(t.n_row_tiles + 1,),
        in_specs=[_cur_rows(t, D_MODEL), _prev_rows(t, D_MODEL), st_spec] + [_const_spec(c.shape) for c in consts],
        out_specs=[_prev_rows(t, D_MODEL), st_spec],
        out_shape=[jax.ShapeDtypeStruct((n_rows, D_MODEL), F32),
                   jax.ShapeDtypeStruct(state.shape, F32)],
        scratch_shapes=[pltpu.VMEM((t.nseg, GLA_HEADS, GLA_DK, GLA_DV), F32),
                        pltpu.VMEM((t.rows, D_V), BF16),
                        pltpu.VMEM((2, t.rows, D_QK), F32),
                        pltpu.VMEM((2, t.rows, D_QK), F32),
                        pltpu.VMEM((2, t.rows, D_V), BF16),
                        pltpu.VMEM((2, t.rows, D_V), F32),
                        pltpu.VMEM((2, t.rows, D_QK), F32)],
        compiler_params=_params(),
        name="gla_mixer_%d" % t.seg_len,
    )(x, x, state, *consts)


def _ffn_call(t, x, w):
    n_rows = x.shape[0]
    consts = (w["gpre"], w["wg"], w["wu"], w["wd"], w["gpost"])
    return pl.pallas_call(
        functools.partial(_ffn_body, t),
        grid=(t.n_row_tiles + 1,),
        in_specs=[_cur_rows(t, D_MODEL), _prev_rows(t, D_MODEL)] + [_const_spec(c.shape) for c in consts],
        out_specs=_prev_rows(t, D_MODEL),
        out_shape=jax.ShapeDtypeStruct((n_rows, D_MODEL), F32),
        scratch_shapes=[pltpu.VMEM((2, t.rows, D_FF), BF16)],
        compiler_params=_params(),
        name="ffn_%d" % t.seg_len,
    )(x, x, *consts)


def _row(g):
    return g.reshape(1, -1)


def _trunk(t, emit_v, x, hist, state, we, wo, wf0, wf1):
    x, hist_out, *v_out = _even_call(t, emit_v, x, hist, we)
    x = _ffn_call(t, x, wf0)
    x, state_out = _gla_call(t, x, state, wo)
    x = _ffn_call(t, x, wf1)
    return x, hist_out, (v_out[0] if emit_v else None), state_out


def kernel(x_prompt, x_sample, state_pool, state_gla, pre_mix_g, post_mix_g, pre_ffn_g, post_ffn_g, w_in_even,
           w_pool, pool_scale, ln_v_g, ln_v_b, w_s, b_s, w_out_even, w_in_odd, w_gate2, b_gate, g_onorm,
           w_out_odd, w_ffn_gate, w_ffn_up, w_ffn_down):
    B, SEQ, _ = x_prompt.shape
    DB, DSEQ, _ = x_sample.shape

    we = dict(gpre=_row(pre_mix_g[0]), win=w_in_even[0].astype(BF16), wpool=w_pool[0].astype(BF16),
              pscale=_row(pool_scale[0]), lng=_row(ln_v_g[0]), lnb=_row(ln_v_b[0]), ws=w_s[0],
              bst=jnp.transpose(b_s[0]), wout=w_out_even[0].astype(BF16), gpost=_row(post_mix_g[0]))
    wi = w_in_odd[0]
    wo = dict(gpre=_row(pre_mix_g[1]), wq=wi[:, :D_QK].astype(BF16), wk=wi[:, D_QK:2 * D_QK].astype(BF16),
              wv=wi[:, 2 * D_QK:2 * D_QK + D_V].astype(BF16),
              wr=wi[:, 2 * D_QK + D_V:2 * D_QK + 2 * D_V].astype(BF16),
              wg1=wi[:, 2 * D_QK + 2 * D_V:].astype(BF16), wg2=w_gate2[0].astype(BF16), bg=_row(b_gate[0]),
              gon=_row(g_onorm[0]), wout=w_out_odd[0].astype(BF16), gpost=_row(post_mix_g[1]))

    def ffn_weights(l):
        return dict(gpre=_row(pre_ffn_g[l]), wg=w_ffn_gate[l].astype(BF16), wu=w_ffn_up[l].astype(BF16),
                    wd=w_ffn_down[l].astype(BF16), gpost=_row(post_ffn_g[l]))

    wf0, wf1 = ffn_weights(0), ffn_weights(1)

    tp = Tiling(nseg=1, seg_len=PROMPT_TILE_ROWS, start=0, n_batch=B, n_tiles=SEQ // PROMPT_TILE_ROWS)
    hist0 = jnp.zeros((B, HIST_ROWS, D_POOL), F32)
    gla0 = jnp.zeros((B, GLA_HEADS, GLA_DK, GLA_DV), F32)
    yp, hp, _, sp = _trunk(tp, False, x_prompt.reshape(B * SEQ, D_MODEL), hist0, gla0, we, wo, wf0, wf1)

    ts = Tiling(nseg=DB, seg_len=DSEQ, start=PAST_LEN, n_batch=1, n_tiles=1)
    hist_s = jnp.pad(state_pool[0], ((0, 0), (HIST_ROWS - POOL_HIST, 0), (0, 0)))
    ys, hs, vs, ss = _trunk(ts, True, x_sample.reshape(DB * DSEQ, D_MODEL), hist_s, state_gla[0], we, wo, wf0, wf1)

    drop = HIST_ROWS - POOL_HIST
    return (yp.reshape(B, SEQ, D_MODEL), ys.reshape(DB, DSEQ, D_MODEL),
            hp[None, :, drop:], hs[None, :, drop:], vs.reshape(1, DB, DSEQ, D_SG), sp[None], ss[None])
```

```python
import functools
from typing import NamedTuple

import jax
import jax.numpy as jnp
from jax import lax
from jax.experimental import pallas as pl
from jax.experimental.pallas import tpu as pltpu

D_MODEL = 1024
PAST_LEN = 1024
CHUNK = 64
POOL_WINDOWS = (2, 4, 8, 16)
D_POOL = 512
POOL_GROUP = 128
POOL_HIST = 15
HIST_ROWS = 16
SG_CHUNK = 128
D_SG = 512
SG_HEADS = 4
SG_HEAD = 128
D_IN_EVEN = D_POOL + 2 * D_SG
GLA_HEADS = 4
GLA_DK = 128
GLA_DV = 256
GLA_GATE_RANK = 16
GLA_TAU = 16.0
D_QK = 512
D_V = 1024
D_FF = 2816
EPS = 1e-6
BLOCK_ROWS = 128
FFN_BLOCK_ROWS = 256
PROMPT_TILE_ROWS = 512
MXU_COLS = 256

F32 = jnp.float32
BF16 = jnp.bfloat16
V7X_VMEM_LIMIT_BYTES = 56 * 1024 * 1024


class Tiling(NamedTuple):
    nseg: int
    seg_len: int
    start: int
    n_batch: int
    n_tiles: int

    @property
    def rows(self):
        return self.nseg * self.seg_len

    @property
    def n_row_tiles(self):
        return self.n_batch * self.n_tiles


def _rms(x, g):
    ms = jnp.mean(x * x, axis=-1, keepdims=True)
    return x * lax.rsqrt(ms + EPS) * g


def _gelu(x):
    c = 0.7978845608028654
    return x * (0.5 * (1.0 + jnp.tanh(c * (x + 0.044715 * (x * x * x)))))


def _silu(x):
    return x / (1.0 + jnp.exp(-x))


def _dot(a, b):
    return jnp.dot(a, b, preferred_element_type=F32)


def _interleave(a, b):
    if len(a) < len(b):
        a, b = b, a
    out, nxt = [], 0
    for n, f in enumerate(a):
        out.append(f)
        while nxt < len(b) and (nxt + 1) * len(a) <= (n + 1) * len(b):
            out.append(b[nxt])
            nxt += 1
    return out + b[nxt:]


def _mixer_schedule(blocks, project_steps, mix_steps, out_steps):
    n = len(blocks)
    states = [None] * n
    states[0], order = project_steps(blocks[0])
    for i in range(n):
        free = []
        if i + 1 < n:
            states[i + 1], free = project_steps(blocks[i + 1])
        if i > 0:
            free = _interleave(free, out_steps(blocks[i - 1], states[i - 1]))
        order = order + _interleave(mix_steps(blocks[i], states[i]), free)
    return order + out_steps(blocks[n - 1], states[n - 1])


def _out_steps(y_in, wout_ref, gpost_ref, x_ref, x1, w, rows):
    ys = []

    def unit(n):
        ys.append(_dot(y_in[rows, :], wout_ref[:, n * MXU_COLS:(n + 1) * MXU_COLS]))

    def finish():
        x1[w, rows, :] = x_ref[rows, :] + _rms(jnp.concatenate(ys, axis=-1), gpost_ref[...])

    return [functools.partial(unit, n) for n in range(D_MODEL // MXU_COLS)] + [finish]


def _ffn_steps(t, x1, r, fw, xo_ref, act):
    gpre_ref, wg_ref, wu_ref, wd_ref, gpost_ref = fw
    nb = min(FFN_BLOCK_ROWS, t.rows)
    blocks = [slice(rb, rb + nb) for rb in range(0, t.rows, nb)]
    hs, ys = {}, {}

    def norm_in(i):
        hs[i] = _rms(x1[r, blocks[i], :], gpre_ref[...]).astype(BF16)

    def up(i, j):
        cols = slice(j * MXU_COLS, (j + 1) * MXU_COLS)
        act[blocks[i], cols] = (_silu(_dot(hs[i], wg_ref[:, cols])) * _dot(hs[i], wu_ref[:, cols])).astype(BF16)

    def down(i, n):
        ys.setdefault(i, []).append(_dot(act[blocks[i], :], wd_ref[:, n * MXU_COLS:(n + 1) * MXU_COLS]))

    def finish(i):
        xo_ref[blocks[i], :] = x1[r, blocks[i], :] + _rms(jnp.concatenate(ys[i], axis=-1), gpost_ref[...])

    ups = [[functools.partial(norm_in, i)] + [functools.partial(up, i, j) for j in range(D_FF // MXU_COLS)]
           for i in range(len(blocks))]
    downs = [[functools.partial(down, i, n) for n in range(D_MODEL // MXU_COLS)] + [functools.partial(finish, i)]
             for i in range(len(blocks))]
    return sum(ups, []) + sum(downs, [])


def _on_parity(g, step):
    for slot in (0, 1):
        pl.when(g % 2 == slot)(functools.partial(step, slot, 1 - slot))


def _even_body(t, emit_v, x_ref, hist_ref, gpre_ref, win_ref, wpool_ref, pscale_ref, lng_ref, lnb_ref, ws_ref,
               bst_ref, wout_ref, gpost_ref, fgpre_ref, wg_ref, wu_ref, wd_ref, fgpost_ref, xo_ref, histo_ref,
               *rest):
    if emit_v:
        vo_ref, x1, act, abuf, ycat, wmask = rest
    else:
        x1, act, abuf, ycat, wmask = rest
    g = pl.program_id(0)
    last = t.n_row_tiles - 1
    tpos = jnp.minimum(g, last) % t.n_tiles
    L = t.seg_len
    nb = min(BLOCK_ROWS, t.rows)
    c = min(L, nb)

    @pl.when(g == 0)
    def _():
        x1[1] = jnp.zeros(x1.shape[1:], F32)
        tril = (lax.broadcasted_iota(jnp.int32, (c, c), 0) >= lax.broadcasted_iota(jnp.int32, (c, c), 1))
        for hd in range(SG_HEADS):
            wmask[hd] = jnp.where(tril, ws_ref[hd, :c, :c], 0.0).astype(BF16)

    @pl.when(tpos == 0)
    def _():
        abuf[:, 0:HIST_ROWS, :] = hist_ref[...]

    biases = [bst_ref[:c, hd:hd + 1] for hd in range(SG_HEADS)]
    iota_c = lax.broadcasted_iota(jnp.int32, (c, 1), 0)

    def pieces(rb):
        return [(p0 // L, p0 % L, slice(p0, p0 + c)) for p0 in range(rb, rb + nb, c)]

    def project_steps(rb):
        st = {"z": {}}
        rows = slice(rb, rb + nb)

        def unit(n):
            if n == 0:
                st["h"] = _rms(x_ref[rows, :], gpre_ref[...]).astype(BF16)
            st["z"][n] = _dot(st["h"], win_ref[:, n * MXU_COLS:(n + 1) * MXU_COLS])

        def pool_rows():
            st["a"] = a = jnp.concatenate([st["z"][0], st["z"][1]], axis=-1)
            for seg, off, pr in pieces(rb):
                abuf[seg, HIST_ROWS + off:HIST_ROWS + off + c, :] = a[pr.start - rb:pr.stop - rb]

        def gate_u():
            st["u"] = _gelu(jnp.concatenate([st["z"][2], st["z"][3]], axis=-1))

        def gate_v():
            gv = _gelu(jnp.concatenate([st["z"][4], st["z"][5]], axis=-1))
            mu = jnp.mean(gv, axis=-1, keepdims=True)
            dv = gv - mu
            var = jnp.mean(dv * dv, axis=-1, keepdims=True)
            vv = dv * lax.rsqrt(var + EPS) * lng_ref[...] + lnb_ref[...]
            st["vv"] = vv.astype(BF16)
            if emit_v:
                vo_ref[rows, :] = vv

        u = [functools.partial(unit, n) for n in range(D_IN_EVEN // MXU_COLS)]
        return st, [u[0], u[1], pool_rows, u[2], u[3], gate_u, u[4], u[5], gate_v]

    def mix_steps(rb, st):
        def pool(seg, off, pr, gi):
            win = POOL_WINDOWS[gi]
            cols = slice(gi * POOL_GROUP, (gi + 1) * POOL_GROUP)
            base = HIST_ROWS + off
            a = st["a"][pr.start - rb:pr.stop - rb, cols]
            acc = a
            for kk in range(1, win):
                acc = acc + abuf[seg, base - kk:base - kk + c, cols]
            cnt = jnp.minimum(win, t.start + tpos * L + off + iota_c + 1).astype(F32)
            pooled = acc / cnt - a
            ya = _dot(pooled.astype(BF16), wpool_ref[gi]) * pscale_ref[:, cols]
            ycat[pr, cols] = ya.astype(BF16)

        def gate(pr, hd):
            cols = slice(hd * SG_HEAD, (hd + 1) * SG_HEAD)
            br = slice(pr.start - rb, pr.stop - rb)
            mixed = _dot(wmask[hd], st["vv"][br, cols]) + biases[hd]
            yb = st["u"][br, cols] * mixed
            ycat[pr, D_POOL + hd * SG_HEAD:D_POOL + (hd + 1) * SG_HEAD] = yb.astype(BF16)

        steps = []
        for seg, off, pr in pieces(rb):
            steps += [functools.partial(pool, seg, off, pr, gi) for gi in range(len(POOL_WINDOWS))]
            steps += [functools.partial(gate, pr, hd) for hd in range(SG_HEADS)]
        return steps

    def step(w, r):
        def out_steps(rb, st):
            return _out_steps(ycat, wout_ref, gpost_ref, x_ref, x1, w, slice(rb, rb + nb))

        mixer = _mixer_schedule(list(range(0, t.rows, nb)), project_steps, mix_steps, out_steps)
        ffn = _ffn_steps(t, x1, r, (fgpre_ref, wg_ref, wu_ref, wd_ref, fgpost_ref), xo_ref, act)
        for f in _interleave(mixer, ffn):
            f()
        abuf[:, 0:HIST_ROWS, :] = abuf[:, L:L + HIST_ROWS, :]

    _on_parity(g, step)

    @pl.when(g <= last)
    def _():
        histo_ref[...] = abuf[:, 0:HIST_ROWS, :]


def _gla_body(t, x_ref, st_ref, gpre_ref, wq_ref, wk_ref, wv_ref, wr_ref, wg1_ref, wg2_ref, bg_ref, gon_ref,
              wout_ref, gpost_ref, fgpre_ref, wg_ref, wu_ref, wd_ref, fgpost_ref, xo_ref, sto_ref,
              x1, act, S, og):
    g = pl.program_id(0)
    last = t.n_row_tiles - 1
    tpos = jnp.minimum(g, last) % t.n_tiles
    L = t.seg_len
    nb = min(BLOCK_ROWS, t.rows)
    c = min(CHUNK, L)
    cpb = nb // c

    @pl.when(g == 0)
    def _():
        x1[1] = jnp.zeros(x1.shape[1:], F32)

    @pl.when(tpos == 0)
    def _():
        S[...] = st_ref[...]

    ri = lax.broadcasted_iota(jnp.int32, (nb, nb), 0)
    ci = lax.broadcasted_iota(jnp.int32, (nb, nb), 1)
    causal = ((ri // c) == (ci // c)) & (ri >= ci)
    tri = jnp.where(causal, 1.0, 0.0).astype(BF16)

    def project_steps(rb):
        st = {}
        rows = slice(rb, rb + nb)

        def low_rank():
            st["h"] = _rms(x_ref[rows, :], gpre_ref[...]).astype(BF16)
            st["glr"] = _dot(st["h"], wg1_ref[...])

        def log_decay():
            gate = _dot(st["glr"].astype(BF16), wg2_ref[...]) + bg_ref[...]
            la = (jnp.minimum(gate, 0.0) - jnp.log(1.0 + jnp.exp(-jnp.abs(gate)))) * (1.0 / GLA_TAU)
            la_hi = la.astype(BF16)
            rem = la - la_hi.astype(F32)
            la_mid = rem.astype(BF16)
            st["la"] = (la_hi, la_mid, (rem - la_mid.astype(F32)).astype(BF16))

        def cum_decay():
            la_hi, la_mid, la_lo = st["la"]
            st["b"] = _dot(tri, la_hi) + _dot(tri, la_mid) + _dot(tri, la_lo)

        def unit(name, w_ref, n, post):
            def f():
                st.setdefault(name, {})[n] = post(_dot(st["h"], w_ref[:, n * MXU_COLS:(n + 1) * MXU_COLS]))
            return f

        scale_q = lambda z: z * (GLA_DK ** -0.5)
        ident = lambda z: z
        to_bf16 = lambda z: z.astype(BF16)
        return st, ([low_rank, unit("q", wq_ref, 0, scale_q), log_decay, unit("q", wq_ref, 1, scale_q), cum_decay,
                     unit("k", wk_ref, 0, ident), unit("k", wk_ref, 1, ident)]
                    + [unit("v", wv_ref, n, to_bf16) for n in range(D_V // MXU_COLS)]
                    + [unit("r", wr_ref, n, ident) for n in range(D_V // MXU_COLS)])

    def mix_steps(rb, st):
        hv = [dict() for _ in range(GLA_HEADS)]

        def scores(hd):
            kc = slice(hd * GLA_DK, (hd + 1) * GLA_DK)
            m = hv[hd]
            half = slice((hd % 2) * GLA_DK, (hd % 2 + 1) * GLA_DK)
            bh = st["b"][:, kc]
            bT = bh.T
            kT = st["k"][hd // 2][:, half].T
            m["qd"] = qd = (st["q"][hd // 2][:, half] * jnp.exp(bh)).astype(BF16)
            kiT = (kT * jnp.exp(-bT)).astype(BF16)
            m["araw"] = _dot(qd, kiT)
            m["ke"], m["dec"] = [], []
            for j in range(cpb):
                bl = bT[:, (j + 1) * c - 1:(j + 1) * c]
                in_chunk = (ci // c) == j
                m["ke"].append(jnp.where(in_chunk, kT * jnp.exp(bl - bT), 0.0).astype(BF16))
                m["dec"].append(jnp.exp(bl))

        def values(hd):
            m = hv[hd]
            att = jnp.where(causal, m["araw"], 0.0).astype(BF16)
            m["res"] = _dot(jnp.concatenate([att] + m["ke"], axis=0), st["v"][hd])

        def recur(j, hd):
            vc = slice(hd * GLA_DV, (hd + 1) * GLA_DV)
            m = hv[hd]
            seg = (rb + j * c) // L
            cr = slice(j * c, (j + 1) * c)
            Sh = S[seg, hd]
            o = _dot(m["qd"][cr], Sh.astype(BF16)) + m["res"][cr]
            S[seg, hd] = m["dec"][j] * Sh + m["res"][(1 + j) * nb:(2 + j) * nb]
            on = o * lax.rsqrt(jnp.mean(o * o, axis=-1, keepdims=True) + EPS) * gon_ref[:, vc]
            og[rb + j * c:rb + (j + 1) * c, vc] = (on * _silu(st["r"][hd][cr])).astype(BF16)

        heads = range(GLA_HEADS)
        return ([functools.partial(scores, hd) for hd in heads]
                + [functools.partial(values, hd) for hd in heads]
                + [functools.partial(recur, j, hd) for j in range(cpb) for hd in heads])

    def step(w, r):
        def out_steps(rb, st):
            return _out_steps(og, wout_ref, gpost_ref, x_ref, x1, w, slice(rb, rb + nb))

        mixer = _mixer_schedule(list(range(0, t.rows, nb)), project_steps, mix_steps, out_steps)
        ffn = _ffn_steps(t, x1, r, (fgpre_ref, wg_ref, wu_ref, wd_ref, fgpost_ref), xo_ref, act)
        for f in _interleave(mixer, ffn):
            f()

    _on_parity(g, step)

    @pl.when(g <= last)
    def _():
        sto_ref[...] = S[...]


def _const_spec(shape):
    nd = len(shape)
    return pl.BlockSpec(shape, lambda g: (0,) * nd, pipeline_mode=pl.Buffered(1))


def _cur_rows(t, width):
    last = t.n_row_tiles - 1
    return pl.BlockSpec((t.rows, width), lambda g: (jnp.minimum(g, last), 0))


def _prev_rows(t, width):
    return pl.BlockSpec((t.rows, width), lambda g: (jnp.maximum(g - 1, 0), 0))


def _cur_seq(t, shape):
    nd = len(shape)
    last = t.n_row_tiles - 1
    return pl.BlockSpec((t.nseg,) + shape, lambda g: (jnp.minimum(g, last) // t.n_tiles,) + (0,) * nd)


def _params():
    return pltpu.CompilerParams(dimension_semantics=("arbitrary",), vmem_limit_bytes=V7X_VMEM_LIMIT_BYTES)


def _ffn_consts(f):
    return (f["gpre"], f["wg"], f["wu"], f["wd"], f["gpost"])


def _even_call(t, emit_v, x, hist, w, f):
    n_rows = x.shape[0]
    consts = (w["gpre"], w["win"], w["wpool"], w["pscale"], w["lng"], w["lnb"], w["ws"], w["bst"], w["wout"],
              w["gpost"]) + _ffn_consts(f)
    c = min(t.seg_len, BLOCK_ROWS)
    out_shape = [jax.ShapeDtypeStruct((n_rows, D_MODEL), F32),
                 jax.ShapeDtypeStruct(hist.shape, F32)]
    out_specs = [_prev_rows(t, D_MODEL), _cur_seq(t, (HIST_ROWS, D_POOL))]
    if emit_v:
        out_shape.append(jax.ShapeDtypeStruct((n_rows, D_SG), F32))
        out_specs.append(_cur_rows(t, D_SG))
    return pl.pallas_call(
        functools.partial(_even_body, t, emit_v),
        grid=(t.n_row_tiles + 1,),
        in_specs=[_cur_rows(t, D_MODEL), _cur_seq(t, (HIST_ROWS, D_POOL))] + [_const_spec(a.shape) for a in consts],
        out_specs=out_specs,
        out_shape=out_shape,
        scratch_shapes=[pltpu.VMEM((2, t.rows, D_MODEL), F32),
                        pltpu.VMEM((t.rows, D_FF), BF16),
                        pltpu.VMEM((t.nseg, HIST_ROWS + t.seg_len, D_POOL), F32),
                        pltpu.VMEM((t.rows, D_POOL + D_SG), BF16),
                        pltpu.VMEM((SG_HEADS, c, c), BF16)],
        compiler_params=_params(),
        name="even_layer_%d" % t.seg_len,
    )(x, hist, *consts)


def _gla_call(t, x, state, w, f):
    n_rows = x.shape[0]
    consts = (w["gpre"], w["wq"], w["wk"], w["wv"], w["wr"], w["wg1"], w["wg2"], w["bg"], w["gon"], w["wout"],
              w["gpost"]) + _ffn_consts(f)
    st_spec = _cur_seq(t, (GLA_HEADS, GLA_DK, GLA_DV))
    return pl.pallas_call(
        functools.partial(_gla_body, t),
        grid=(t.n_row_tiles + 1,),
        in_specs=[_cur_rows(t, D_MODEL), st_spec] + [_const_spec(a.shape) for a in consts],
        out_specs=[_prev_rows(t, D_MODEL), st_spec],
        out_shape=[jax.ShapeDtypeStruct((n_rows, D_MODEL), F32),
                   jax.ShapeDtypeStruct(state.shape, F32)],
        scratch_shapes=[pltpu.VMEM((2, t.rows, D_MODEL), F32),
                        pltpu.VMEM((t.rows, D_FF), BF16),
                        pltpu.VMEM((t.nseg, GLA_HEADS, GLA_DK, GLA_DV), F32),
                        pltpu.VMEM((t.rows, D_V), BF16)],
        compiler_params=_params(),
        name="gla_layer_%d" % t.seg_len,
    )(x, state, *consts)


def _row(g):
    return g.reshape(1, -1)


def _trunk(t, emit_v, x, hist, state, we, wo, wf0, wf1):
    x, hist_out, *v_out = _even_call(t, emit_v, x, hist, we, wf0)
    x, state_out = _gla_call(t, x, state, wo, wf1)
    return x, hist_out, (v_out[0] if emit_v else None), state_out


def kernel(x_prompt, x_sample, state_pool, state_gla, pre_mix_g, post_mix_g, pre_ffn_g, post_ffn_g, w_in_even,
           w_pool, pool_scale, ln_v_g, ln_v_b, w_s, b_s, w_out_even, w_in_odd, w_gate2, b_gate, g_onorm,
           w_out_odd, w_ffn_gate, w_ffn_up, w_ffn_down):
    B, SEQ, _ = x_prompt.shape
    DB, DSEQ, _ = x_sample.shape

    we = dict(gpre=_row(pre_mix_g[0]), win=w_in_even[0].astype(BF16), wpool=w_pool[0].astype(BF16),
              pscale=_row(pool_scale[0]), lng=_row(ln_v_g[0]), lnb=_row(ln_v_b[0]), ws=w_s[0],
              bst=jnp.transpose(b_s[0]), wout=w_out_even[0].astype(BF16), gpost=_row(post_mix_g[0]))
    wi = w_in_odd[0]
    wo = dict(gpre=_row(pre_mix_g[1]), wq=wi[:, :D_QK].astype(BF16), wk=wi[:, D_QK:2 * D_QK].astype(BF16),
              wv=wi[:, 2 * D_QK:2 * D_QK + D_V].astype(BF16),
              wr=wi[:, 2 * D_QK + D_V:2 * D_QK + 2 * D_V].astype(BF16),
              wg1=wi[:, 2 * D_QK + 2 * D_V:].astype(BF16), wg2=w_gate2[0].astype(BF16), bg=_row(b_gate[0]),
              gon=_row(g_onorm[0]), wout=w_out_odd[0].astype(BF16), gpost=_row(post_mix_g[1]))

    def ffn_weights(l):
        return dict(gpre=_row(pre_ffn_g[l]), wg=w_ffn_gate[l].astype(BF16), wu=w_ffn_up[l].astype(BF16),
                    wd=w_ffn_down[l].astype(BF16), gpost=_row(post_ffn_g[l]))

    wf0, wf1 = ffn_weights(0), ffn_weights(1)

    tp = Tiling(nseg=1, seg_len=PROMPT_TILE_ROWS, start=0, n_batch=B, n_tiles=SEQ // PROMPT_TILE_ROWS)
    hist0 = jnp.zeros((B, HIST_ROWS, D_POOL), F32)
    gla0 = jnp.zeros((B, GLA_HEADS, GLA_DK, GLA_DV), F32)
    yp, hp, _, sp = _trunk(tp, False, x_prompt.reshape(B * SEQ, D_MODEL), hist0, gla0, we, wo, wf0, wf1)

    ts = Tiling(nseg=DB, seg_len=DSEQ, start=PAST_LEN, n_batch=1, n_tiles=1)
    hist_s = jnp.pad(state_pool[0], ((0, 0), (HIST_ROWS - POOL_HIST, 0), (0, 0)))
    ys, hs, vs, ss = _trunk(ts, True, x_sample.reshape(DB * DSEQ, D_MODEL), hist_s, state_gla[0], we, wo, wf0, wf1)

    drop = HIST_ROWS - POOL_HIST
    return (yp.reshape(B, SEQ, D_MODEL), ys.reshape(DB, DSEQ, D_MODEL),
            hp[None, :, drop:], hs[None, :, drop:], vs.reshape(1, DB, DSEQ, D_SG), sp[None], ss[None])
```

```python
import functools
from typing import NamedTuple

import jax
import jax.numpy as jnp
from jax import lax
from jax.experimental import pallas as pl
from jax.experimental.pallas import tpu as pltpu

D_MODEL = 1024
PAST_LEN = 1024
CHUNK = 64
POOL_WINDOWS = (2, 4, 8, 16)
D_POOL = 512
POOL_GROUP = 128
POOL_HIST = 15
HIST_ROWS = 16
SG_CHUNK = 128
D_SG = 512
SG_HEADS = 4
SG_HEAD = 128
D_IN_EVEN = D_POOL + 2 * D_SG
GLA_HEADS = 4
GLA_DK = 128
GLA_DV = 256
GLA_GATE_RANK = 16
GLA_TAU = 16.0
D_QK = 512
D_V = 1024
D_FF = 2816
EPS = 1e-6
BLOCK_ROWS = 128
FFN_BLOCK_ROWS = 256
PROMPT_TILE_ROWS = 512
MXU_COLS = 256

F32 = jnp.float32
BF16 = jnp.bfloat16
V7X_VMEM_LIMIT_BYTES = 56 * 1024 * 1024


class Tiling(NamedTuple):
    nseg: int
    seg_len: int
    start: int
    n_batch: int
    n_tiles: int

    @property
    def rows(self):
        return self.nseg * self.seg_len

    @property
    def n_row_tiles(self):
        return self.n_batch * self.n_tiles

    @property
    def single_step(self):
        return self.n_row_tiles == 1

    @property
    def n_steps(self):
        return 1 if self.single_step else self.n_row_tiles + 1


def _rms(x, g):
    ms = jnp.mean(x * x, axis=-1, keepdims=True)
    return x * lax.rsqrt(ms + EPS) * g


def _gelu(x):
    c = 0.7978845608028654
    return x * (0.5 * (1.0 + jnp.tanh(c * (x + 0.044715 * (x * x * x)))))


def _silu(x):
    return x / (1.0 + jnp.exp(-x))


def _dot(a, b):
    return jnp.dot(a, b, preferred_element_type=F32)


def _interleave(a, b):
    if len(a) < len(b):
        a, b = b, a
    out, nxt = [], 0
    for n, f in enumerate(a):
        out.append(f)
        while nxt < len(b) and (nxt + 1) * len(a) <= (n + 1) * len(b):
            out.append(b[nxt])
            nxt += 1
    return out + b[nxt:]


def _mixer_schedule(blocks, project_steps, mix_steps, out_steps):
    n = len(blocks)
    states = [None] * n
    states[0], first = project_steps(blocks[0])
    segments = [first]
    for i in range(n):
        free = []
        if i + 1 < n:
            states[i + 1], free = project_steps(blocks[i + 1])
        if i > 0:
            free = _interleave(free, out_steps(blocks[i - 1], states[i - 1]))
        segments.append(_interleave(mix_steps(blocks[i], states[i]), free))
    return segments + [out_steps(blocks[n - 1], states[n - 1])]


def _merge_filler(segments, filler):
    weights = [1.0] * (len(segments) - 2) + [2.0, 0.0]
    if len(segments) == 3:
        weights = [1.0, 2.0, 0.0]
    total = sum(weights)
    order, used, acc = [], 0, 0.0
    for seg, wgt in zip(segments, weights):
        acc += wgt
        upto = len(filler) if acc >= total else int(round(len(filler) * acc / total))
        order += _interleave(seg, filler[used:upto])
        used = upto
    return order


def _out_steps(y_in, wout_ref, gpost_ref, x_ref, x1, w, rows):
    ys = []

    def unit(n):
        ys.append(_dot(y_in[rows, :], wout_ref[:, n * MXU_COLS:(n + 1) * MXU_COLS]))

    def finish():
        x1[w, rows, :] = x_ref[rows, :] + _rms(jnp.concatenate(ys, axis=-1), gpost_ref[...])

    return [functools.partial(unit, n) for n in range(D_MODEL // MXU_COLS)] + [finish]


def _ffn_steps(t, x1, r, fw, xo_ref, act):
    gpre_ref, wg_ref, wu_ref, wd_ref, gpost_ref = fw
    nb = min(FFN_BLOCK_ROWS, t.rows)
    blocks = [slice(rb, rb + nb) for rb in range(0, t.rows, nb)]
    hs, ys = {}, {}

    def norm_in(i):
        hs[i] = _rms(x1[r, blocks[i], :], gpre_ref[...]).astype(BF16)

    def up(i, j):
        cols = slice(j * MXU_COLS, (j + 1) * MXU_COLS)
        act[blocks[i], cols] = (_silu(_dot(hs[i], wg_ref[:, cols])) * _dot(hs[i], wu_ref[:, cols])).astype(BF16)

    def down(i, n):
        ys.setdefault(i, []).append(_dot(act[blocks[i], :], wd_ref[:, n * MXU_COLS:(n + 1) * MXU_COLS]))

    def finish(i):
        xo_ref[blocks[i], :] = x1[r, blocks[i], :] + _rms(jnp.concatenate(ys[i], axis=-1), gpost_ref[...])

    ups = [[functools.partial(norm_in, i)] + [functools.partial(up, i, j) for j in range(D_FF // MXU_COLS)]
           for i in range(len(blocks))]
    downs = [[functools.partial(down, i, n) for n in range(D_MODEL // MXU_COLS)] + [functools.partial(finish, i)]
             for i in range(len(blocks))]
    return sum(ups, []) + sum(downs, [])


def _even_body(t, emit_v, x_ref, hist_ref, gpre_ref, win_ref, wpool_ref, pscale_ref, lng_ref, lnb_ref, ws_ref,
               bst_ref, wout_ref, gpost_ref, fgpre_ref, wg_ref, wu_ref, wd_ref, fgpost_ref, xo_ref, histo_ref,
               *rest):
    if emit_v:
        vo_ref, x1, act, abuf, ycat, wmask = rest
    else:
        x1, act, abuf, ycat, wmask = rest
    g = pl.program_id(0)
    last = t.n_row_tiles - 1
    tpos = jnp.minimum(g, last) % t.n_tiles
    L = t.seg_len
    nb = min(BLOCK_ROWS, t.rows)
    c = min(L, nb)

    @pl.when(g == 0)
    def _():
        x1[1] = jnp.zeros(x1.shape[1:], F32)
        tril = (lax.broadcasted_iota(jnp.int32, (c, c), 0) >= lax.broadcasted_iota(jnp.int32, (c, c), 1))
        for hd in range(SG_HEADS):
            wmask[hd] = jnp.where(tril, ws_ref[hd, :c, :c], 0.0).astype(BF16)

    @pl.when(tpos == 0)
    def _():
        abuf[:, 0:HIST_ROWS, :] = hist_ref[...]

    biases = [bst_ref[:c, hd:hd + 1] for hd in range(SG_HEADS)]
    iota_c = lax.broadcasted_iota(jnp.int32, (c, 1), 0)

    def pieces(rb):
        return [(p0 // L, p0 % L, slice(p0, p0 + c)) for p0 in range(rb, rb + nb, c)]

    def project_steps(rb):
        st = {"z": {}}
        rows = slice(rb, rb + nb)

        def unit(n):
            if n == 0:
                st["h"] = _rms(x_ref[rows, :], gpre_ref[...]).astype(BF16)
            st["z"][n] = _dot(st["h"], win_ref[:, n * MXU_COLS:(n + 1) * MXU_COLS])

        def pool_rows():
            st["a"] = a = jnp.concatenate([st["z"][0], st["z"][1]], axis=-1)
            for seg, off, pr in pieces(rb):
                abuf[seg, HIST_ROWS + off:HIST_ROWS + off + c, :] = a[pr.start - rb:pr.stop - rb]

        def gate_u():
            st["u"] = _gelu(jnp.concatenate([st["z"][2], st["z"][3]], axis=-1))

        def gate_v():
            gv = _gelu(jnp.concatenate([st["z"][4], st["z"][5]], axis=-1))
            mu = jnp.mean(gv, axis=-1, keepdims=True)
            dv = gv - mu
            var = jnp.mean(dv * dv, axis=-1, keepdims=True)
            vv = dv * lax.rsqrt(var + EPS) * lng_ref[...] + lnb_ref[...]
            st["vv"] = vv.astype(BF16)
            if emit_v:
                vo_ref[rows, :] = vv

        u = [functools.partial(unit, n) for n in range(D_IN_EVEN // MXU_COLS)]
        return st, [u[0], u[1], pool_rows, u[2], u[3], gate_u, u[4], u[5], gate_v]

    def mix_steps(rb, st):
        def pool(seg, off, pr, gi):
            win = POOL_WINDOWS[gi]
            cols = slice(gi * POOL_GROUP, (gi + 1) * POOL_GROUP)
            base = HIST_ROWS + off
            a = st["a"][pr.start - rb:pr.stop - rb, cols]
            acc = a
            for kk in range(1, win):
                acc = acc + abuf[seg, base - kk:base - kk + c, cols]
            cnt = jnp.minimum(win, t.start + tpos * L + off + iota_c + 1).astype(F32)
            pooled = acc / cnt - a
            ya = _dot(pooled.astype(BF16), wpool_ref[gi]) * pscale_ref[:, cols]
            ycat[pr, cols] = ya.astype(BF16)

        def gate(pr, hd):
            cols = slice(hd * SG_HEAD, (hd + 1) * SG_HEAD)
            br = slice(pr.start - rb, pr.stop - rb)
            mixed = _dot(wmask[hd], st["vv"][br, cols]) + biases[hd]
            yb = st["u"][br, cols] * mixed
            ycat[pr, D_POOL + hd * SG_HEAD:D_POOL + (hd + 1) * SG_HEAD] = yb.astype(BF16)

        steps = []
        for seg, off, pr in pieces(rb):
            steps += [functools.partial(pool, seg, off, pr, gi) for gi in range(len(POOL_WINDOWS))]
            steps += [functools.partial(gate, pr, hd) for hd in range(SG_HEADS)]
        return steps

    def step(w, r):
        def out_steps(rb, st):
            return _out_steps(ycat, wout_ref, gpost_ref, x_ref, x1, w, slice(rb, rb + nb))

        mixer = _mixer_schedule(list(range(0, t.rows, nb)), project_steps, mix_steps, out_steps)
        ffn = _ffn_steps(t, x1, r, (fgpre_ref, wg_ref, wu_ref, wd_ref, fgpost_ref), xo_ref, act)
        for f in (sum(mixer, []) + ffn if t.single_step else _merge_filler(mixer, ffn)):
            f()
        abuf[:, 0:HIST_ROWS, :] = abuf[:, L:L + HIST_ROWS, :]

    if t.single_step:
        step(0, 0)
    else:
        step(g % 2, 1 - g % 2)

    @pl.when(g <= last)
    def _():
        histo_ref[...] = abuf[:, 0:HIST_ROWS, :]


def _gla_body(t, x_ref, st_ref, gpre_ref, wq_ref, wk_ref, wv_ref, wr_ref, wg1_ref, wg2_ref, bg_ref, gon_ref,
              wout_ref, gpost_ref, fgpre_ref, wg_ref, wu_ref, wd_ref, fgpost_ref, xo_ref, sto_ref,
              x1, act, S, og):
    g = pl.program_id(0)
    last = t.n_row_tiles - 1
    tpos = jnp.minimum(g, last) % t.n_tiles
    L = t.seg_len
    nb = min(BLOCK_ROWS, t.rows)
    c = min(CHUNK, L)
    cpb = nb // c

    @pl.when(g == 0)
    def _():
        x1[1] = jnp.zeros(x1.shape[1:], F32)

    @pl.when(tpos == 0)
    def _():
        S[...] = st_ref[...]

    ri = lax.broadcasted_iota(jnp.int32, (nb, nb), 0)
    ci = lax.broadcasted_iota(jnp.int32, (nb, nb), 1)
    causal = ((ri // c) == (ci // c)) & (ri >= ci)
    tri = jnp.where(causal, 1.0, 0.0).astype(BF16)

    def project_steps(rb):
        st = {}
        rows = slice(rb, rb + nb)

        def low_rank():
            st["h"] = _rms(x_ref[rows, :], gpre_ref[...]).astype(BF16)
            st["glr"] = _dot(st["h"], wg1_ref[...])

        def log_decay():
            gate = _dot(st["glr"].astype(BF16), wg2_ref[...]) + bg_ref[...]
            la = (jnp.minimum(gate, 0.0) - jnp.log(1.0 + jnp.exp(-jnp.abs(gate)))) * (1.0 / GLA_TAU)
            la_hi = la.astype(BF16)
            rem = la - la_hi.astype(F32)
            la_mid = rem.astype(BF16)
            st["la"] = (la_hi, la_mid, (rem - la_mid.astype(F32)).astype(BF16))

        def cum_decay():
            la_hi, la_mid, la_lo = st["la"]
            st["b"] = b = _dot(tri, la_hi) + _dot(tri, la_mid) + _dot(tri, la_lo)
            st["bT"] = [b[:, hd * GLA_DK:(hd + 1) * GLA_DK].T for hd in range(GLA_HEADS)]

        def heads_of(n):
            return [(2 * n + e, slice(e * GLA_DK, (e + 1) * GLA_DK)) for e in range(MXU_COLS // GLA_DK)]

        def query(n):
            z = _dot(st["h"], wq_ref[:, n * MXU_COLS:(n + 1) * MXU_COLS]) * (GLA_DK ** -0.5)
            for hd, half in heads_of(n):
                bh = st["b"][:, hd * GLA_DK:(hd + 1) * GLA_DK]
                st["qd", hd] = (z[:, half] * jnp.exp(bh)).astype(BF16)

        def key(n):
            z = _dot(st["h"], wk_ref[:, n * MXU_COLS:(n + 1) * MXU_COLS])
            for hd, half in heads_of(n):
                bT = st["bT"][hd]
                kT = z[:, half].T
                st["kiT", hd] = (kT * jnp.exp(-bT)).astype(BF16)
                ke, dec = [], []
                for j in range(cpb):
                    bl = bT[:, (j + 1) * c - 1:(j + 1) * c]
                    in_chunk = (ci // c) == j
                    ke.append(jnp.where(in_chunk, kT * jnp.exp(bl - bT), 0.0).astype(BF16))
                    dec.append(jnp.exp(bl))
                st["ke", hd], st["dec", hd] = ke, dec

        def unit(name, w_ref, n, post):
            def f():
                st[name, n] = post(_dot(st["h"], w_ref[:, n * MXU_COLS:(n + 1) * MXU_COLS]))
            return f

        ident = lambda z: z
        to_bf16 = lambda z: z.astype(BF16)
        vals = [unit("v", wv_ref, n, to_bf16) for n in range(D_V // MXU_COLS)]
        outg = [unit("r", wr_ref, n, ident) for n in range(D_V // MXU_COLS)]
        return st, ([low_rank, vals[0], log_decay, vals[1], cum_decay, vals[2]]
                    + [functools.partial(query, 0), functools.partial(key, 0), functools.partial(query, 1),
                       functools.partial(key, 1), vals[3]] + outg)

    def mix_steps(rb, st):
        res = {}

        def scores(hd):
            res["araw", hd] = _dot(st["qd", hd], st["kiT", hd])

        def values(hd):
            att = jnp.where(causal, res["araw", hd], 0.0).astype(BF16)
            res[hd] = _dot(jnp.concatenate([att] + st["ke", hd], axis=0), st["v", hd])

        def recur(j, hd):
            vc = slice(hd * GLA_DV, (hd + 1) * GLA_DV)
            seg = (rb + j * c) // L
            cr = slice(j * c, (j + 1) * c)
            Sh = S[seg, hd]
            o = _dot(st["qd", hd][cr], Sh.astype(BF16)) + res[hd][cr]
            S[seg, hd] = st["dec", hd][j] * Sh + res[hd][(1 + j) * nb:(2 + j) * nb]
            on = o * lax.rsqrt(jnp.mean(o * o, axis=-1, keepdims=True) + EPS) * gon_ref[:, vc]
            og[rb + j * c:rb + (j + 1) * c, vc] = (on * _silu(st["r", hd][cr])).astype(BF16)

        heads = range(GLA_HEADS)
        return ([functools.partial(scores, hd) for hd in heads]
                + [functools.partial(values, hd) for hd in heads]
                + [functools.partial(recur, j, hd) for j in range(cpb) for hd in heads])

    def step(w, r):
        def out_steps(rb, st):
            return _out_steps(og, wout_ref, gpost_ref, x_ref, x1, w, slice(rb, rb + nb))

        mixer = _mixer_schedule(list(range(0, t.rows, nb)), project_steps, mix_steps, out_steps)
        ffn = _ffn_steps(t, x1, r, (fgpre_ref, wg_ref, wu_ref, wd_ref, fgpost_ref), xo_ref, act)
        for f in (sum(mixer, []) + ffn if t.single_step else _merge_filler(mixer, ffn)):
            f()

    if t.single_step:
        step(0, 0)
    else:
        step(g % 2, 1 - g % 2)

    @pl.when(g <= last)
    def _():
        sto_ref[...] = S[...]


def _const_spec(shape):
    nd = len(shape)
    return pl.BlockSpec(shape, lambda g: (0,) * nd, pipeline_mode=pl.Buffered(1))


def _cur_rows(t, width):
    last = t.n_row_tiles - 1
    return pl.BlockSpec((t.rows, width), lambda g: (jnp.minimum(g, last), 0))


def _prev_rows(t, width):
    return pl.BlockSpec((t.rows, width), lambda g: (jnp.maximum(g - 1, 0), 0))


def _cur_seq(t, shape):
    nd = len(shape)
    last = t.n_row_tiles - 1
    return pl.BlockSpec((t.nseg,) + shape, lambda g: (jnp.minimum(g, last) // t.n_tiles,) + (0,) * nd)


def _params():
    return pltpu.CompilerParams(dimension_semantics=("arbitrary",), vmem_limit_bytes=V7X_VMEM_LIMIT_BYTES)


def _ffn_consts(f):
    return (f["gpre"], f["wg"], f["wu"], f["wd"], f["gpost"])


def _even_call(t, emit_v, x, hist, w, f):
    n_rows = x.shape[0]
    consts = (w["gpre"], w["win"], w["wpool"], w["pscale"], w["lng"], w["lnb"], w["ws"], w["bst"], w["wout"],
              w["gpost"]) + _ffn_consts(f)
    c = min(t.seg_len, BLOCK_ROWS)
    out_shape = [jax.ShapeDtypeStruct((n_rows, D_MODEL), F32),
                 jax.ShapeDtypeStruct(hist.shape, F32)]
    out_specs = [_prev_rows(t, D_MODEL), _cur_seq(t, (HIST_ROWS, D_POOL))]
    if emit_v:
        out_shape.append(jax.ShapeDtypeStruct((n_rows, D_SG), F32))
        out_specs.append(_cur_rows(t, D_SG))
    return pl.pallas_call(
        functools.partial(_even_body, t, emit_v),
        grid=(t.n_steps,),
        in_specs=[_cur_rows(t, D_MODEL), _cur_seq(t, (HIST_ROWS, D_POOL))] + [_const_spec(a.shape) for a in consts],
        out_specs=out_specs,
        out_shape=out_shape,
        scratch_shapes=[pltpu.VMEM((2, t.rows, D_MODEL), F32),
                        pltpu.VMEM((t.rows, D_FF), BF16),
                        pltpu.VMEM((t.nseg, HIST_ROWS + t.seg_len, D_POOL), F32),
                        pltpu.VMEM((t.rows, D_POOL + D_SG), BF16),
                        pltpu.VMEM((SG_HEADS, c, c), BF16)],
        compiler_params=_params(),
        name="even_layer_%d" % t.seg_len,
    )(x, hist, *consts)


def _gla_call(t, x, state, w, f):
    n_rows = x.shape[0]
    consts = (w["gpre"], w["wq"], w["wk"], w["wv"], w["wr"], w["wg1"], w["wg2"], w["bg"], w["gon"], w["wout"],
              w["gpost"]) + _ffn_consts(f)
    st_spec = _cur_seq(t, (GLA_HEADS, GLA_DK, GLA_DV))
    return pl.pallas_call(
        functools.partial(_gla_body, t),
        grid=(t.n_steps,),
        in_specs=[_cur_rows(t, D_MODEL), st_spec] + [_const_spec(a.shape) for a in consts],
        out_specs=[_prev_rows(t, D_MODEL), st_spec],
        out_shape=[jax.ShapeDtypeStruct((n_rows, D_MODEL), F32),
                   jax.ShapeDtypeStruct(state.shape, F32)],
        scratch_shapes=[pltpu.VMEM((2, t.rows, D_MODEL), F32),
                        pltpu.VMEM((t.rows, D_FF), BF16),
                        pltpu.VMEM((t.nseg, GLA_HEADS, GLA_DK, GLA_DV), F32),
                        pltpu.VMEM((t.rows, D_V), BF16)],
        compiler_params=_params(),
        name="gla_layer_%d" % t.seg_len,
    )(x, state, *consts)


def _row(g):
    return g.reshape(1, -1)


def _trunk(t, emit_v, x, hist, state, we, wo, wf0, wf1):
    x, hist_out, *v_out = _even_call(t, emit_v, x, hist, we, wf0)
    x, state_out = _gla_call(t, x, state, wo, wf1)
    return x, hist_out, (v_out[0] if emit_v else None), state_out


def kernel(x_prompt, x_sample, state_pool, state_gla, pre_mix_g, post_mix_g, pre_ffn_g, post_ffn_g, w_in_even,
           w_pool, pool_scale, ln_v_g, ln_v_b, w_s, b_s, w_out_even, w_in_odd, w_gate2, b_gate, g_onorm,
           w_out_odd, w_ffn_gate, w_ffn_up, w_ffn_down):
    B, SEQ, _ = x_prompt.shape
    DB, DSEQ, _ = x_sample.shape

    we = dict(gpre=_row(pre_mix_g[0]), win=w_in_even[0].astype(BF16), wpool=w_pool[0].astype(BF16),
              pscale=_row(pool_scale[0]), lng=_row(ln_v_g[0]), lnb=_row(ln_v_b[0]), ws=w_s[0],
              bst=jnp.transpose(b_s[0]), wout=w_out_even[0].astype(BF16), gpost=_row(post_mix_g[0]))
    wi = w_in_odd[0]
    wo = dict(gpre=_row(pre_mix_g[1]), wq=wi[:, :D_QK].astype(BF16), wk=wi[:, D_QK:2 * D_QK].astype(BF16),
              wv=wi[:, 2 * D_QK:2 * D_QK + D_V].astype(BF16),
              wr=wi[:, 2 * D_QK + D_V:2 * D_QK + 2 * D_V].astype(BF16),
              wg1=wi[:, 2 * D_QK + 2 * D_V:].astype(BF16), wg2=w_gate2[0].astype(BF16), bg=_row(b_gate[0]),
              gon=_row(g_onorm[0]), wout=w_out_odd[0].astype(BF16), gpost=_row(post_mix_g[1]))

    def ffn_weights(l):
        return dict(gpre=_row(pre_ffn_g[l]), wg=w_ffn_gate[l].astype(BF16), wu=w_ffn_up[l].astype(BF16),
                    wd=w_ffn_down[l].astype(BF16), gpost=_row(post_ffn_g[l]))

    wf0, wf1 = ffn_weights(0), ffn_weights(1)

    tp = Tiling(nseg=1, seg_len=PROMPT_TILE_ROWS, start=0, n_batch=B, n_tiles=SEQ // PROMPT_TILE_ROWS)
    hist0 = jnp.zeros((B, HIST_ROWS, D_POOL), F32)
    gla0 = jnp.zeros((B, GLA_HEADS, GLA_DK, GLA_DV), F32)
    yp, hp, _, sp = _trunk(tp, False, x_prompt.reshape(B * SEQ, D_MODEL), hist0, gla0, we, wo, wf0, wf1)

    ts = Tiling(nseg=DB, seg_len=DSEQ, start=PAST_LEN, n_batch=1, n_tiles=1)
    hist_s = jnp.pad(state_pool[0], ((0, 0), (HIST_ROWS - POOL_HIST, 0), (0, 0)))
    ys, hs, vs, ss = _trunk(ts, True, x_sample.reshape(DB * DSEQ, D_MODEL), hist_s, state_gla[0], we, wo, wf0, wf1)

    drop = HIST_ROWS - POOL_HIST
    return (yp.reshape(B, SEQ, D_MODEL), ys.reshape(DB, DSEQ, D_MODEL),
            hp[None, :, drop:], hs[None, :, drop:], vs.reshape(1, DB, DSEQ, D_SG), sp[None], ss[None])
```

```python
import functools
from typing import NamedTuple

import jax
import jax.numpy as jnp
from jax import lax
from jax.experimental import pallas as pl
from jax.experimental.pallas import tpu as pltpu

D_MODEL = 1024
PAST_LEN = 1024
CHUNK = 64
POOL_WINDOWS = (2, 4, 8, 16)
D_POOL = 512
POOL_GROUP = 128
POOL_HIST = 15
HIST_ROWS = 16
SG_CHUNK = 128
D_SG = 512
SG_HEADS = 4
SG_HEAD = 128
D_IN_EVEN = D_POOL + 2 * D_SG
GLA_HEADS = 4
GLA_DK = 128
GLA_DV = 256
GLA_GATE_RANK = 16
GLA_TAU = 16.0
D_QK = 512
D_V = 1024
OFF_Q, OFF_K, OFF_V, OFF_R, OFF_GLR = 0, D_QK, 2 * D_QK, 2 * D_QK + D_V, 2 * D_QK + 2 * D_V
D_FF = 2816
EPS = 1e-6
BLOCK_ROWS = 128
FFN_BLOCK_ROWS = 256
PROMPT_TILE_ROWS = 512
MXU_COLS = 256

F32 = jnp.float32
BF16 = jnp.bfloat16
V7X_VMEM_LIMIT_BYTES = 56 * 1024 * 1024


class Tiling(NamedTuple):
    nseg: int
    seg_len: int
    start: int
    n_batch: int
    n_tiles: int

    @property
    def rows(self):
        return self.nseg * self.seg_len

    @property
    def n_row_tiles(self):
        return self.n_batch * self.n_tiles

    @property
    def single_step(self):
        return self.n_row_tiles == 1

    @property
    def n_steps(self):
        return 1 if self.single_step else self.n_row_tiles + 1


def _rms(x, g):
    ms = jnp.mean(x * x, axis=-1, keepdims=True)
    return x * lax.rsqrt(ms + EPS) * g


def _gelu(x):
    c = 0.7978845608028654
    return x * (0.5 * (1.0 + jnp.tanh(c * (x + 0.044715 * (x * x * x)))))


def _silu(x):
    return x / (1.0 + jnp.exp(-x))


def _dot(a, b):
    return jnp.dot(a, b, preferred_element_type=F32)


def _interleave(a, b):
    if len(a) < len(b):
        a, b = b, a
    out, nxt = [], 0
    for n, f in enumerate(a):
        out.append(f)
        while nxt < len(b) and (nxt + 1) * len(a) <= (n + 1) * len(b):
            out.append(b[nxt])
            nxt += 1
    return out + b[nxt:]


def _mixer_schedule(blocks, project_steps, mix_steps, out_steps):
    n = len(blocks)
    states = [None] * n
    states[0], first = project_steps(blocks[0])
    segments = [first]
    for i in range(n):
        free = []
        if i + 1 < n:
            states[i + 1], free = project_steps(blocks[i + 1])
        if i > 0:
            free = _interleave(free, out_steps(blocks[i - 1], states[i - 1]))
        segments.append(_interleave(mix_steps(blocks[i], states[i]), free))
    return segments + [out_steps(blocks[n - 1], states[n - 1])]


def _merge_filler(segments, filler):
    weights = [1.0] * (len(segments) - 2) + [2.0, 0.0]
    if len(segments) == 3:
        weights = [1.0, 2.0, 0.0]
    total = sum(weights)
    order, used, acc = [], 0, 0.0
    for seg, wgt in zip(segments, weights):
        acc += wgt
        upto = len(filler) if acc >= total else int(round(len(filler) * acc / total))
        order += _interleave(seg, filler[used:upto])
        used = upto
    return order


def _out_steps(y_in, wout_ref, gpost_ref, x_ref, x1, w, rows):
    ys = []

    def unit(n):
        ys.append(_dot(y_in[rows, :], wout_ref[:, n * MXU_COLS:(n + 1) * MXU_COLS]))

    def finish():
        x1[w, rows, :] = x_ref[rows, :] + _rms(jnp.concatenate(ys, axis=-1), gpost_ref[...])

    return [functools.partial(unit, n) for n in range(D_MODEL // MXU_COLS)] + [finish]


def _ffn_steps(t, x1, r, fw, xo_ref, act):
    gpre_ref, wg_ref, wu_ref, wd_ref, gpost_ref = fw
    nb = min(FFN_BLOCK_ROWS, t.rows)
    blocks = [slice(rb, rb + nb) for rb in range(0, t.rows, nb)]
    hs, ys = {}, {}

    def norm_in(i):
        hs[i] = _rms(x1[r, blocks[i], :], gpre_ref[...]).astype(BF16)

    def up(i, j):
        cols = slice(j * MXU_COLS, (j + 1) * MXU_COLS)
        act[blocks[i], cols] = (_silu(_dot(hs[i], wg_ref[:, cols])) * _dot(hs[i], wu_ref[:, cols])).astype(BF16)

    def down(i, n):
        ys.setdefault(i, []).append(_dot(act[blocks[i], :], wd_ref[:, n * MXU_COLS:(n + 1) * MXU_COLS]))

    def finish(i):
        xo_ref[blocks[i], :] = x1[r, blocks[i], :] + _rms(jnp.concatenate(ys[i], axis=-1), gpost_ref[...])

    ups = [[functools.partial(norm_in, i)] + [functools.partial(up, i, j) for j in range(D_FF // MXU_COLS)]
           for i in range(len(blocks))]
    downs = [[functools.partial(down, i, n) for n in range(D_MODEL // MXU_COLS)] + [functools.partial(finish, i)]
             for i in range(len(blocks))]
    return sum(ups, []) + sum(downs, [])


def _even_body(t, emit_v, x_ref, hist_ref, gpre_ref, win_ref, wpool_ref, pscale_ref, lng_ref, lnb_ref, ws_ref,
               bst_ref, wout_ref, gpost_ref, fgpre_ref, wg_ref, wu_ref, wd_ref, fgpost_ref, xo_ref, histo_ref,
               *rest):
    if emit_v:
        vo_ref, x1, act, abuf, ycat, wmask, wpool2 = rest
    else:
        x1, act, abuf, ycat, wmask, wpool2 = rest
    g = pl.program_id(0)
    last = t.n_row_tiles - 1
    tpos = jnp.minimum(g, last) % t.n_tiles
    L = t.seg_len
    nb = min(BLOCK_ROWS, t.rows)
    c = min(L, nb)
    pair_heads = c == SG_CHUNK

    @pl.when(g == 0)
    def _():
        x1[1] = jnp.zeros(x1.shape[1:], F32)
        tril = (lax.broadcasted_iota(jnp.int32, (c, c), 0) >= lax.broadcasted_iota(jnp.int32, (c, c), 1))
        for hd in range(SG_HEADS):
            wmask[hd] = jnp.where(tril, ws_ref[hd, :c, :c], 0.0).astype(BF16)
        wpool2[...] = jnp.zeros(wpool2.shape, BF16)
        for gi in range(len(POOL_WINDOWS)):
            d = slice((gi % 2) * POOL_GROUP, (gi % 2 + 1) * POOL_GROUP)
            wpool2[gi // 2, d, d] = wpool_ref[gi]

    @pl.when(tpos == 0)
    def _():
        abuf[:, 0:HIST_ROWS, :] = hist_ref[...]

    biases = [bst_ref[:c, hd:hd + 1] for hd in range(SG_HEADS)]
    iota_c = lax.broadcasted_iota(jnp.int32, (c, 1), 0)

    def pieces(rb):
        return [(p0 // L, p0 % L, slice(p0, p0 + c)) for p0 in range(rb, rb + nb, c)]

    def project_steps(rb):
        st = {"z": {}}
        rows = slice(rb, rb + nb)

        def unit(n):
            if n == 0:
                st["h"] = _rms(x_ref[rows, :], gpre_ref[...]).astype(BF16)
            st["z"][n] = _dot(st["h"], win_ref[:, n * MXU_COLS:(n + 1) * MXU_COLS])

        def pool_rows():
            st["a"] = a = jnp.concatenate([st["z"][0], st["z"][1]], axis=-1)
            for seg, off, pr in pieces(rb):
                abuf[seg, HIST_ROWS + off:HIST_ROWS + off + c, :] = a[pr.start - rb:pr.stop - rb]

        def gate_u():
            st["u"] = _gelu(jnp.concatenate([st["z"][2], st["z"][3]], axis=-1))

        def gate_v():
            gv = _gelu(jnp.concatenate([st["z"][4], st["z"][5]], axis=-1))
            mu = jnp.mean(gv, axis=-1, keepdims=True)
            dv = gv - mu
            var = jnp.mean(dv * dv, axis=-1, keepdims=True)
            vv = dv * lax.rsqrt(var + EPS) * lng_ref[...] + lnb_ref[...]
            st["vv"] = vv.astype(BF16)
            if emit_v:
                vo_ref[rows, :] = vv

        u = [functools.partial(unit, n) for n in range(D_IN_EVEN // MXU_COLS)]
        return st, [u[0], u[1], pool_rows, u[2], u[3], gate_u, u[4], u[5], gate_v]

    def mix_steps(rb, st):
        def window_mean(seg, off, pr, gi):
            win = POOL_WINDOWS[gi]
            cols = slice(gi * POOL_GROUP, (gi + 1) * POOL_GROUP)
            base = HIST_ROWS + off
            a = st["a"][pr.start - rb:pr.stop - rb, cols]
            acc = a
            for kk in range(1, win):
                acc = acc + abuf[seg, base - kk:base - kk + c, cols]
            cnt = jnp.minimum(win, t.start + tpos * L + off + iota_c + 1).astype(F32)
            return (acc / cnt - a).astype(BF16)

        def pool(seg, off, pr, p):
            cols = slice(2 * p * POOL_GROUP, (2 * p + 2) * POOL_GROUP)
            pooled = jnp.concatenate([window_mean(seg, off, pr, 2 * p), window_mean(seg, off, pr, 2 * p + 1)], axis=1)
            ya = _dot(pooled, wpool2[p]) * pscale_ref[:, cols]
            ycat[pr, cols] = ya.astype(BF16)

        def gate(pr, hd):
            cols = slice(hd * SG_HEAD, (hd + 1) * SG_HEAD)
            br = slice(pr.start - rb, pr.stop - rb)
            mixed = _dot(wmask[hd], st["vv"][br, cols]) + biases[hd]
            yb = st["u"][br, cols] * mixed
            ycat[pr, D_POOL + hd * SG_HEAD:D_POOL + (hd + 1) * SG_HEAD] = yb.astype(BF16)

        def gate_pair(pr, p):
            cols = slice(2 * p * SG_HEAD, (2 * p + 2) * SG_HEAD)
            br = slice(pr.start - rb, pr.stop - rb)
            va = st["vv"][br, 2 * p * SG_HEAD:(2 * p + 1) * SG_HEAD]
            vb = st["vv"][br, (2 * p + 1) * SG_HEAD:(2 * p + 2) * SG_HEAD]
            zero = jnp.zeros_like(va)
            vals = jnp.concatenate([jnp.concatenate([va, zero], axis=1), jnp.concatenate([zero, vb], axis=1)], axis=0)
            wpair = jnp.concatenate([wmask[2 * p], wmask[2 * p + 1]], axis=1)
            bias = jnp.concatenate([jnp.broadcast_to(biases[2 * p], (c, SG_HEAD)),
                                    jnp.broadcast_to(biases[2 * p + 1], (c, SG_HEAD))], axis=1)
            yb = st["u"][br, cols] * (_dot(wpair, vals) + bias)
            ycat[pr, D_POOL + cols.start:D_POOL + cols.stop] = yb.astype(BF16)

        steps = []
        for seg, off, pr in pieces(rb):
            steps += [functools.partial(pool, seg, off, pr, p) for p in range(len(POOL_WINDOWS) // 2)]
            if pair_heads:
                steps += [functools.partial(gate_pair, pr, p) for p in range(SG_HEADS // 2)]
            else:
                steps += [functools.partial(gate, pr, hd) for hd in range(SG_HEADS)]
        return steps

    def step(w, r):
        def out_steps(rb, st):
            return _out_steps(ycat, wout_ref, gpost_ref, x_ref, x1, w, slice(rb, rb + nb))

        mixer = _mixer_schedule(list(range(0, t.rows, nb)), project_steps, mix_steps, out_steps)
        ffn = _ffn_steps(t, x1, r, (fgpre_ref, wg_ref, wu_ref, wd_ref, fgpost_ref), xo_ref, act)
        for f in (sum(mixer, []) + ffn if t.single_step else _merge_filler(mixer, ffn)):
            f()
        abuf[:, 0:HIST_ROWS, :] = abuf[:, L:L + HIST_ROWS, :]

    if t.single_step:
        step(0, 0)
    else:
        step(g % 2, 1 - g % 2)

    @pl.when(g <= last)
    def _():
        histo_ref[...] = abuf[:, 0:HIST_ROWS, :]


def _gla_body(t, x_ref, st_ref, gpre_ref, win_ref, wg2_ref, bg_ref, gon_ref,
              wout_ref, gpost_ref, fgpre_ref, wg_ref, wu_ref, wd_ref, fgpost_ref, xo_ref, sto_ref,
              x1, act, S, og):
    g = pl.program_id(0)
    last = t.n_row_tiles - 1
    tpos = jnp.minimum(g, last) % t.n_tiles
    L = t.seg_len
    nb = min(BLOCK_ROWS, t.rows)
    c = min(CHUNK, L)
    cpb = nb // c

    @pl.when(g == 0)
    def _():
        x1[1] = jnp.zeros(x1.shape[1:], F32)

    @pl.when(tpos == 0)
    def _():
        S[...] = st_ref[...]

    ri = lax.broadcasted_iota(jnp.int32, (nb, nb), 0)
    ci = lax.broadcasted_iota(jnp.int32, (nb, nb), 1)
    causal = ((ri // c) == (ci // c)) & (ri >= ci)
    tri = jnp.where(causal, 1.0, 0.0).astype(BF16)

    def project_steps(rb):
        st = {}
        rows = slice(rb, rb + nb)

        def low_rank():
            st["h"] = _rms(x_ref[rows, :], gpre_ref[...]).astype(BF16)
            st["glr"] = _dot(st["h"], win_ref[:, OFF_GLR:OFF_GLR + GLA_GATE_RANK])

        def log_decay():
            gate = _dot(st["glr"].astype(BF16), wg2_ref[...]) + bg_ref[...]
            la = (jnp.minimum(gate, 0.0) - jnp.log(1.0 + jnp.exp(-jnp.abs(gate)))) * (1.0 / GLA_TAU)
            la_hi = la.astype(BF16)
            rem = la - la_hi.astype(F32)
            la_mid = rem.astype(BF16)
            st["la"] = (la_hi, la_mid, (rem - la_mid.astype(F32)).astype(BF16))

        def cum_decay():
            la_hi, la_mid, la_lo = st["la"]
            st["b"] = b = _dot(tri, la_hi) + _dot(tri, la_mid) + _dot(tri, la_lo)
            st["bT"] = [b[:, hd * GLA_DK:(hd + 1) * GLA_DK].T for hd in range(GLA_HEADS)]

        def heads_of(n):
            return [(2 * n + e, slice(e * GLA_DK, (e + 1) * GLA_DK)) for e in range(MXU_COLS // GLA_DK)]

        def query(n):
            cols = slice(n * MXU_COLS, (n + 1) * MXU_COLS)
            z = _dot(st["h"], win_ref[:, OFF_Q + cols.start:OFF_Q + cols.stop]) * (GLA_DK ** -0.5)
            st["qd2", n] = qd2 = (z * jnp.exp(st["b"][:, cols])).astype(BF16)
            for hd, half in heads_of(n):
                st["qd", hd] = qd2[:, half]

        def key(n):
            z = _dot(st["h"], win_ref[:, OFF_K + n * MXU_COLS:OFF_K + (n + 1) * MXU_COLS])
            for hd, half in heads_of(n):
                bT = st["bT"][hd]
                kT = z[:, half].T
                st["kiT", hd] = (kT * jnp.exp(-bT)).astype(BF16)
                ke, dec = [], []
                for j in range(cpb):
                    bl = bT[:, (j + 1) * c - 1:(j + 1) * c]
                    in_chunk = (ci // c) == j
                    ke.append(jnp.where(in_chunk, kT * jnp.exp(bl - bT), 0.0).astype(BF16))
                    dec.append(jnp.exp(bl))
                st["ke", hd], st["dec", hd] = ke, dec

        def unit(name, off, n, post):
            def f():
                st[name, n] = post(_dot(st["h"], win_ref[:, off + n * MXU_COLS:off + (n + 1) * MXU_COLS]))
            return f

        ident = lambda z: z
        to_bf16 = lambda z: z.astype(BF16)
        vals = [unit("v", OFF_V, n, to_bf16) for n in range(D_V // MXU_COLS)]
        outg = [unit("r", OFF_R, n, ident) for n in range(D_V // MXU_COLS)]
        return st, ([low_rank, vals[0], log_decay, vals[1], cum_decay, vals[2]]
                    + [functools.partial(query, 0), functools.partial(key, 0), functools.partial(query, 1),
                       functools.partial(key, 1), vals[3]] + outg)

    def mix_steps(rb, st):
        res = {}

        def scores(n):
            ha, hb = 2 * n, 2 * n + 1
            zero = jnp.zeros((GLA_DK, nb), BF16)
            keys = jnp.concatenate([jnp.concatenate([st["kiT", ha], zero], axis=1),
                                    jnp.concatenate([zero, st["kiT", hb]], axis=1)], axis=0)
            raw = _dot(st["qd2", n], keys)
            res["araw", ha], res["araw", hb] = raw[:, :nb], raw[:, nb:]

        def values(hd):
            att = jnp.where(causal, res["araw", hd], 0.0).astype(BF16)
            res[hd] = _dot(jnp.concatenate([att] + st["ke", hd], axis=0), st["v", hd])

        def recur(j, hd):
            vc = slice(hd * GLA_DV, (hd + 1) * GLA_DV)
            seg = (rb + j * c) // L
            cr = slice(j * c, (j + 1) * c)
            Sh = S[seg, hd]
            o = _dot(st["qd", hd][cr], Sh.astype(BF16)) + res[hd][cr]
            S[seg, hd] = st["dec", hd][j] * Sh + res[hd][(1 + j) * nb:(2 + j) * nb]
            on = o * lax.rsqrt(jnp.mean(o * o, axis=-1, keepdims=True) + EPS) * gon_ref[:, vc]
            og[rb + j * c:rb + (j + 1) * c, vc] = (on * _silu(st["r", hd][cr])).astype(BF16)

        heads = range(GLA_HEADS)
        return ([functools.partial(scores, n) for n in range(GLA_HEADS // 2)]
                + [functools.partial(values, hd) for hd in heads]
                + [functools.partial(recur, j, hd) for j in range(cpb) for hd in heads])

    def step(w, r):
        def out_steps(rb, st):
            return _out_steps(og, wout_ref, gpost_ref, x_ref, x1, w, slice(rb, rb + nb))

        mixer = _mixer_schedule(list(range(0, t.rows, nb)), project_steps, mix_steps, out_steps)
        ffn = _ffn_steps(t, x1, r, (fgpre_ref, wg_ref, wu_ref, wd_ref, fgpost_ref), xo_ref, act)
        for f in (sum(mixer, []) + ffn if t.single_step else _merge_filler(mixer, ffn)):
            f()

    if t.single_step:
        step(0, 0)
    else:
        step(g % 2, 1 - g % 2)

    @pl.when(g <= last)
    def _():
        sto_ref[...] = S[...]


def _const_spec(shape):
    nd = len(shape)
    return pl.BlockSpec(shape, lambda g: (0,) * nd, pipeline_mode=pl.Buffered(1))


def _cur_rows(t, width):
    last = t.n_row_tiles - 1
    return pl.BlockSpec((t.rows, width), lambda g: (jnp.minimum(g, last), 0))


def _prev_rows(t, width):
    return pl.BlockSpec((t.rows, width), lambda g: (jnp.maximum(g - 1, 0), 0))


def _cur_seq(t, shape):
    nd = len(shape)
    last = t.n_row_tiles - 1
    return pl.BlockSpec((t.nseg,) + shape, lambda g: (jnp.minimum(g, last) // t.n_tiles,) + (0,) * nd)


def _params():
    return pltpu.CompilerParams(dimension_semantics=("arbitrary",), vmem_limit_bytes=V7X_VMEM_LIMIT_BYTES)


def _ffn_consts(f):
    return (f["gpre"], f["wg"], f["wu"], f["wd"], f["gpost"])


def _even_call(t, emit_v, x, hist, w, f):
    n_rows = x.shape[0]
    consts = (w["gpre"], w["win"], w["wpool"], w["pscale"], w["lng"], w["lnb"], w["ws"], w["bst"], w["wout"],
              w["gpost"]) + _ffn_consts(f)
    c = min(t.seg_len, BLOCK_ROWS)
    out_shape = [jax.ShapeDtypeStruct((n_rows, D_MODEL), F32),
                 jax.ShapeDtypeStruct(hist.shape, F32)]
    out_specs = [_prev_rows(t, D_MODEL), _cur_seq(t, (HIST_ROWS, D_POOL))]
    if emit_v:
        out_shape.append(jax.ShapeDtypeStruct((n_rows, D_SG), F32))
        out_specs.append(_cur_rows(t, D_SG))
    return pl.pallas_call(
        functools.partial(_even_body, t, emit_v),
        grid=(t.n_steps,),
        in_specs=[_cur_rows(t, D_MODEL), _cur_seq(t, (HIST_ROWS, D_POOL))] + [_const_spec(a.shape) for a in consts],
        out_specs=out_specs,
        out_shape=out_shape,
        scratch_shapes=[pltpu.VMEM((2, t.rows, D_MODEL), F32),
                        pltpu.VMEM((t.rows, D_FF), BF16),
                        pltpu.VMEM((t.nseg, HIST_ROWS + t.seg_len, D_POOL), F32),
                        pltpu.VMEM((t.rows, D_POOL + D_SG), BF16),
                        pltpu.VMEM((SG_HEADS, c, c), BF16),
                        pltpu.VMEM((len(POOL_WINDOWS) // 2, MXU_COLS, MXU_COLS), BF16)],
        compiler_params=_params(),
        name="even_layer_%d" % t.seg_len,
    )(x, hist, *consts)


def _gla_call(t, x, state, w, f):
    n_rows = x.shape[0]
    consts = (w["gpre"], w["win"], w["wg2"], w["bg"], w["gon"], w["wout"],
              w["gpost"]) + _ffn_consts(f)
    st_spec = _cur_seq(t, (GLA_HEADS, GLA_DK, GLA_DV))
    return pl.pallas_call(
        functools.partial(_gla_body, t),
        grid=(t.n_steps,),
        in_specs=[_cur_rows(t, D_MODEL), st_spec] + [_const_spec(a.shape) for a in consts],
        out_specs=[_prev_rows(t, D_MODEL), st_spec],
        out_shape=[jax.ShapeDtypeStruct((n_rows, D_MODEL), F32),
                   jax.ShapeDtypeStruct(state.shape, F32)],
        scratch_shapes=[pltpu.VMEM((2, t.rows, D_MODEL), F32),
                        pltpu.VMEM((t.rows, D_FF), BF16),
                        pltpu.VMEM((t.nseg, GLA_HEADS, GLA_DK, GLA_DV), F32),
                        pltpu.VMEM((t.rows, D_V), BF16)],
        compiler_params=_params(),
        name="gla_layer_%d" % t.seg_len,
    )(x, state, *consts)


def _row(g):
    return g.reshape(1, -1)


def _trunk(t, emit_v, x, hist, state, we, wo, wf0, wf1):
    x, hist_out, *v_out = _even_call(t, emit_v, x, hist, we, wf0)
    x, state_out = _gla_call(t, x, state, wo, wf1)
    return x, hist_out, (v_out[0] if emit_v else None), state_out


def kernel(x_prompt, x_sample, state_pool, state_gla, pre_mix_g, post_mix_g, pre_ffn_g, post_ffn_g, w_in_even,
           w_pool, pool_scale, ln_v_g, ln_v_b, w_s, b_s, w_out_even, w_in_odd, w_gate2, b_gate, g_onorm,
           w_out_odd, w_ffn_gate, w_ffn_up, w_ffn_down):
    B, SEQ, _ = x_prompt.shape
    DB, DSEQ, _ = x_sample.shape

    we = dict(gpre=_row(pre_mix_g[0]), win=w_in_even[0].astype(BF16), wpool=w_pool[0].astype(BF16),
              pscale=_row(pool_scale[0]), lng=_row(ln_v_g[0]), lnb=_row(ln_v_b[0]), ws=w_s[0],
              bst=jnp.transpose(b_s[0]), wout=w_out_even[0].astype(BF16), gpost=_row(post_mix_g[0]))
    wo = dict(gpre=_row(pre_mix_g[1]), win=w_in_odd[0].astype(BF16), wg2=w_gate2[0].astype(BF16), bg=_row(b_gate[0]),
              gon=_row(g_onorm[0]), wout=w_out_odd[0].astype(BF16), gpost=_row(post_mix_g[1]))

    def ffn_weights(l):
        return dict(gpre=_row(pre_ffn_g[l]), wg=w_ffn_gate[l].astype(BF16), wu=w_ffn_up[l].astype(BF16),
                    wd=w_ffn_down[l].astype(BF16), gpost=_row(post_ffn_g[l]))

    wf0, wf1 = ffn_weights(0), ffn_weights(1)

    tp = Tiling(nseg=1, seg_len=PROMPT_TILE_ROWS, start=0, n_batch=B, n_tiles=SEQ // PROMPT_TILE_ROWS)
    hist0 = jnp.zeros((B, HIST_ROWS, D_POOL), F32)
    gla0 = jnp.zeros((B, GLA_HEADS, GLA_DK, GLA_DV), F32)
    yp, hp, _, sp = _trunk(tp, False, x_prompt.reshape(B * SEQ, D_MODEL), hist0, gla0, we, wo, wf0, wf1)

    ts = Tiling(nseg=DB, seg_len=DSEQ, start=PAST_LEN, n_batch=1, n_tiles=1)
    hist_s = jnp.pad(state_pool[0], ((0, 0), (HIST_ROWS - POOL_HIST, 0), (0, 0)))
    ys, hs, vs, ss = _trunk(ts, True, x_sample.reshape(DB * DSEQ, D_MODEL), hist_s, state_gla[0], we, wo, wf0, wf1)

    drop = HIST_ROWS - POOL_HIST
    return (yp.reshape(B, SEQ, D_MODEL), ys.reshape(DB, DSEQ, D_MODEL),
            hp[None, :, drop:], hs[None, :, drop:], vs.reshape(1, DB, DSEQ, D_SG), sp[None], ss[None])
```

```python
import functools
from typing import NamedTuple

import jax
import jax.numpy as jnp
from jax import lax
from jax.experimental import pallas as pl
from jax.experimental.pallas import tpu as pltpu

D_MODEL = 1024
PAST_LEN = 1024
CHUNK = 64
POOL_WINDOWS = (2, 4, 8, 16)
D_POOL = 512
POOL_GROUP = 128
POOL_HIST = 15
HIST_ROWS = 16
SG_CHUNK = 128
D_SG = 512
SG_HEADS = 4
SG_HEAD = 128
D_IN_EVEN = D_POOL + 2 * D_SG
GLA_HEADS = 4
GLA_DK = 128
GLA_DV = 256
GLA_GATE_RANK = 16
GLA_TAU = 16.0
D_QK = 512
D_V = 1024
OFF_Q, OFF_K, OFF_V, OFF_R, OFF_GLR = 0, D_QK, 2 * D_QK, 2 * D_QK + D_V, 2 * D_QK + 2 * D_V
D_FF = 2816
EPS = 1e-6
BLOCK_ROWS = 128
FFN_BLOCK_ROWS = 256
PROMPT_TILE_ROWS = 512
MXU_COLS = 256

F32 = jnp.float32
BF16 = jnp.bfloat16
V7X_VMEM_LIMIT_BYTES = 56 * 1024 * 1024


class Tiling(NamedTuple):
    nseg: int
    seg_len: int
    start: int
    n_batch: int
    n_tiles: int

    @property
    def rows(self):
        return self.nseg * self.seg_len

    @property
    def n_row_tiles(self):
        return self.n_batch * self.n_tiles

    @property
    def single_step(self):
        return self.n_row_tiles == 1

    @property
    def n_steps(self):
        return 1 if self.single_step else self.n_row_tiles + 1


def _rms(x, g):
    ms = jnp.mean(x * x, axis=-1, keepdims=True)
    return x * lax.rsqrt(ms + EPS) * g


def _gelu(x):
    c = 0.7978845608028654
    return x * (0.5 * (1.0 + jnp.tanh(c * (x + 0.044715 * (x * x * x)))))


def _silu(x):
    return x / (1.0 + jnp.exp(-x))


def _dot(a, b):
    return jnp.dot(a, b, preferred_element_type=F32)


def _interleave(a, b):
    if len(a) < len(b):
        a, b = b, a
    out, nxt = [], 0
    for n, f in enumerate(a):
        out.append(f)
        while nxt < len(b) and (nxt + 1) * len(a) <= (n + 1) * len(b):
            out.append(b[nxt])
            nxt += 1
    return out + b[nxt:]


def _mixer_schedule(blocks, project_steps, mix_steps, out_steps):
    n = len(blocks)
    states = [None] * n
    states[0], first = project_steps(blocks[0])
    segments = [first]
    for i in range(n):
        free = []
        if i + 1 < n:
            states[i + 1], free = project_steps(blocks[i + 1])
        if i > 0:
            free = _interleave(free, out_steps(blocks[i - 1], states[i - 1]))
        segments.append(_interleave(mix_steps(blocks[i], states[i]), free))
    return segments + [out_steps(blocks[n - 1], states[n - 1])]


def _merge_filler(segments, filler):
    weights = [1.0] * (len(segments) - 2) + [2.0, 0.0]
    if len(segments) == 3:
        weights = [1.0, 2.0, 0.0]
    total = sum(weights)
    order, used, acc = [], 0, 0.0
    for seg, wgt in zip(segments, weights):
        acc += wgt
        upto = len(filler) if acc >= total else int(round(len(filler) * acc / total))
        order += _interleave(seg, filler[used:upto])
        used = upto
    return order


def _out_steps(y_in, wout_ref, gpost_ref, x_ref, x1, w, rows):
    ys = []

    def unit(n):
        ys.append(_dot(y_in[rows, :], wout_ref[:, n * MXU_COLS:(n + 1) * MXU_COLS]))

    def finish():
        x1[w, rows, :] = x_ref[rows, :] + _rms(jnp.concatenate(ys, axis=-1), gpost_ref[...])

    return [functools.partial(unit, n) for n in range(D_MODEL // MXU_COLS)] + [finish]


def _ffn_steps(t, x1, r, fw, xo_ref, act):
    gpre_ref, wg_ref, wu_ref, wd_ref, gpost_ref = fw
    nb = min(FFN_BLOCK_ROWS, t.rows)
    blocks = [slice(rb, rb + nb) for rb in range(0, t.rows, nb)]
    hs, ys = {}, {}

    def norm_in(i):
        hs[i] = _rms(x1[r, blocks[i], :], gpre_ref[...]).astype(BF16)

    def up(i, j):
        cols = slice(j * MXU_COLS, (j + 1) * MXU_COLS)
        act[blocks[i], cols] = (_silu(_dot(hs[i], wg_ref[:, cols])) * _dot(hs[i], wu_ref[:, cols])).astype(BF16)

    def down(i, n):
        ys.setdefault(i, []).append(_dot(act[blocks[i], :], wd_ref[:, n * MXU_COLS:(n + 1) * MXU_COLS]))

    def finish(i):
        xo_ref[blocks[i], :] = x1[r, blocks[i], :] + _rms(jnp.concatenate(ys[i], axis=-1), gpost_ref[...])

    ups = [[functools.partial(norm_in, i)] + [functools.partial(up, i, j) for j in range(D_FF // MXU_COLS)]
           for i in range(len(blocks))]
    downs = [[functools.partial(down, i, n) for n in range(D_MODEL // MXU_COLS)] + [functools.partial(finish, i)]
             for i in range(len(blocks))]
    return sum(ups, []) + sum(downs, [])


def _even_body(t, emit_v, x_ref, hist_ref, gpre_ref, win_ref, wpool_ref, pscale_ref, lng_ref, lnb_ref, ws_ref,
               bst_ref, wout_ref, gpost_ref, fgpre_ref, wg_ref, wu_ref, wd_ref, fgpost_ref, xo_ref, histo_ref,
               *rest):
    if emit_v:
        vo_ref, x1, act, abuf, ycat, wmask, wpool2 = rest
    else:
        x1, act, abuf, ycat, wmask, wpool2 = rest
    g = pl.program_id(0)
    last = t.n_row_tiles - 1
    tpos = jnp.minimum(g, last) % t.n_tiles
    L = t.seg_len
    nb = min(BLOCK_ROWS, t.rows)
    c = min(L, nb)
    pair_heads = c == SG_CHUNK

    @pl.when(g == 0)
    def _():
        x1[1] = jnp.zeros(x1.shape[1:], F32)
        tril = (lax.broadcasted_iota(jnp.int32, (c, c), 0) >= lax.broadcasted_iota(jnp.int32, (c, c), 1))
        for hd in range(SG_HEADS):
            wmask[hd] = jnp.where(tril, ws_ref[hd, :c, :c], 0.0).astype(BF16)
        wpool2[...] = jnp.zeros(wpool2.shape, BF16)
        for gi in range(len(POOL_WINDOWS)):
            d = slice((gi % 2) * POOL_GROUP, (gi % 2 + 1) * POOL_GROUP)
            wpool2[gi // 2, d, d] = wpool_ref[gi]

    @pl.when(tpos == 0)
    def _():
        abuf[:, 0:HIST_ROWS, :] = jnp.zeros((t.nseg, HIST_ROWS, D_POOL), F32) if hist_ref is None else hist_ref[...]

    biases = [bst_ref[:c, hd:hd + 1] for hd in range(SG_HEADS)]
    iota_c = lax.broadcasted_iota(jnp.int32, (c, 1), 0)

    def pieces(rb):
        return [(p0 // L, p0 % L, slice(p0, p0 + c)) for p0 in range(rb, rb + nb, c)]

    def project_steps(rb):
        st = {"z": {}}
        rows = slice(rb, rb + nb)

        def unit(n):
            if n == 0:
                st["h"] = _rms(x_ref[rows, :], gpre_ref[...]).astype(BF16)
            st["z"][n] = _dot(st["h"], win_ref[:, n * MXU_COLS:(n + 1) * MXU_COLS])

        def pool_rows():
            st["a"] = a = jnp.concatenate([st["z"][0], st["z"][1]], axis=-1)
            for seg, off, pr in pieces(rb):
                abuf[seg, HIST_ROWS + off:HIST_ROWS + off + c, :] = a[pr.start - rb:pr.stop - rb]

        def gate_u():
            st["u"] = _gelu(jnp.concatenate([st["z"][2], st["z"][3]], axis=-1))

        def gate_v():
            gv = _gelu(jnp.concatenate([st["z"][4], st["z"][5]], axis=-1))
            mu = jnp.mean(gv, axis=-1, keepdims=True)
            dv = gv - mu
            var = jnp.mean(dv * dv, axis=-1, keepdims=True)
            vv = dv * lax.rsqrt(var + EPS) * lng_ref[...] + lnb_ref[...]
            st["vv"] = vv.astype(BF16)
            if emit_v:
                vo_ref[rows, :] = vv

        u = [functools.partial(unit, n) for n in range(D_IN_EVEN // MXU_COLS)]
        return st, [u[0], u[1], pool_rows, u[2], u[3], gate_u, u[4], u[5], gate_v]

    def mix_steps(rb, st):
        def window_mean(seg, off, pr, gi):
            win = POOL_WINDOWS[gi]
            cols = slice(gi * POOL_GROUP, (gi + 1) * POOL_GROUP)
            base = HIST_ROWS + off
            a = st["a"][pr.start - rb:pr.stop - rb, cols]
            acc = a
            for kk in range(1, win):
                acc = acc + abuf[seg, base - kk:base - kk + c, cols]
            cnt = jnp.minimum(win, t.start + tpos * L + off + iota_c + 1).astype(F32)
            return (acc / cnt - a).astype(BF16)

        def pool(seg, off, pr, p):
            cols = slice(2 * p * POOL_GROUP, (2 * p + 2) * POOL_GROUP)
            pooled = jnp.concatenate([window_mean(seg, off, pr, 2 * p), window_mean(seg, off, pr, 2 * p + 1)], axis=1)
            ya = _dot(pooled, wpool2[p]) * pscale_ref[:, cols]
            ycat[pr, cols] = ya.astype(BF16)

        def gate(pr, hd):
            cols = slice(hd * SG_HEAD, (hd + 1) * SG_HEAD)
            br = slice(pr.start - rb, pr.stop - rb)
            mixed = _dot(wmask[hd], st["vv"][br, cols]) + biases[hd]
            yb = st["u"][br, cols] * mixed
            ycat[pr, D_POOL + hd * SG_HEAD:D_POOL + (hd + 1) * SG_HEAD] = yb.astype(BF16)

        def gate_pair(pr, p):
            cols = slice(2 * p * SG_HEAD, (2 * p + 2) * SG_HEAD)
            br = slice(pr.start - rb, pr.stop - rb)
            va = st["vv"][br, 2 * p * SG_HEAD:(2 * p + 1) * SG_HEAD]
            vb = st["vv"][br, (2 * p + 1) * SG_HEAD:(2 * p + 2) * SG_HEAD]
            zero = jnp.zeros_like(va)
            vals = jnp.concatenate([jnp.concatenate([va, zero], axis=1), jnp.concatenate([zero, vb], axis=1)], axis=0)
            wpair = jnp.concatenate([wmask[2 * p], wmask[2 * p + 1]], axis=1)
            bias = jnp.concatenate([jnp.broadcast_to(biases[2 * p], (c, SG_HEAD)),
                                    jnp.broadcast_to(biases[2 * p + 1], (c, SG_HEAD))], axis=1)
            yb = st["u"][br, cols] * (_dot(wpair, vals) + bias)
            ycat[pr, D_POOL + cols.start:D_POOL + cols.stop] = yb.astype(BF16)

        steps = []
        for seg, off, pr in pieces(rb):
            steps += [functools.partial(pool, seg, off, pr, p) for p in range(len(POOL_WINDOWS) // 2)]
            if pair_heads:
                steps += [functools.partial(gate_pair, pr, p) for p in range(SG_HEADS // 2)]
            else:
                steps += [functools.partial(gate, pr, hd) for hd in range(SG_HEADS)]
        return steps

    def step(w, r):
        def out_steps(rb, st):
            return _out_steps(ycat, wout_ref, gpost_ref, x_ref, x1, w, slice(rb, rb + nb))

        mixer = _mixer_schedule(list(range(0, t.rows, nb)), project_steps, mix_steps, out_steps)
        ffn = _ffn_steps(t, x1, r, (fgpre_ref, wg_ref, wu_ref, wd_ref, fgpost_ref), xo_ref, act)
        for f in (sum(mixer, []) + ffn if t.single_step else _merge_filler(mixer, ffn)):
            f()
        abuf[:, 0:HIST_ROWS, :] = abuf[:, L:L + HIST_ROWS, :]

    if t.single_step:
        step(0, 0)
    else:
        step(g % 2, 1 - g % 2)

    @pl.when(g <= last)
    def _():
        histo_ref[...] = abuf[:, 0:HIST_ROWS, :]


def _gla_body(t, x_ref, st_ref, gpre_ref, win_ref, wg2_ref, bg_ref, gon_ref,
              wout_ref, gpost_ref, fgpre_ref, wg_ref, wu_ref, wd_ref, fgpost_ref, xo_ref, sto_ref,
              x1, act, S, og):
    g = pl.program_id(0)
    last = t.n_row_tiles - 1
    tpos = jnp.minimum(g, last) % t.n_tiles
    L = t.seg_len
    nb = min(BLOCK_ROWS, t.rows)
    c = min(CHUNK, L)
    cpb = nb // c

    @pl.when(g == 0)
    def _():
        x1[1] = jnp.zeros(x1.shape[1:], F32)

    @pl.when(tpos == 0)
    def _():
        S[...] = jnp.zeros(S.shape, F32) if st_ref is None else st_ref[...]

    ri = lax.broadcasted_iota(jnp.int32, (nb, nb), 0)
    ci = lax.broadcasted_iota(jnp.int32, (nb, nb), 1)
    causal = ((ri // c) == (ci // c)) & (ri >= ci)
    tri = jnp.where(causal, 1.0, 0.0).astype(BF16)

    def project_steps(rb):
        st = {}
        rows = slice(rb, rb + nb)

        def low_rank():
            st["h"] = _rms(x_ref[rows, :], gpre_ref[...]).astype(BF16)
            st["glr"] = _dot(st["h"], win_ref[:, OFF_GLR:OFF_GLR + GLA_GATE_RANK])

        def log_decay():
            gate = _dot(st["glr"].astype(BF16), wg2_ref[...]) + bg_ref[...]
            la = (jnp.minimum(gate, 0.0) - jnp.log(1.0 + jnp.exp(-jnp.abs(gate)))) * (1.0 / GLA_TAU)
            la_hi = la.astype(BF16)
            rem = la - la_hi.astype(F32)
            la_mid = rem.astype(BF16)
            st["la"] = (la_hi, la_mid, (rem - la_mid.astype(F32)).astype(BF16))

        def cum_decay():
            la_hi, la_mid, la_lo = st["la"]
            st["b"] = b = _dot(tri, la_hi) + _dot(tri, la_mid) + _dot(tri, la_lo)
            st["bT"] = [b[:, hd * GLA_DK:(hd + 1) * GLA_DK].T for hd in range(GLA_HEADS)]

        def heads_of(n):
            return [(2 * n + e, slice(e * GLA_DK, (e + 1) * GLA_DK)) for e in range(MXU_COLS // GLA_DK)]

        def query(n):
            cols = slice(n * MXU_COLS, (n + 1) * MXU_COLS)
            z = _dot(st["h"], win_ref[:, OFF_Q + cols.start:OFF_Q + cols.stop]) * (GLA_DK ** -0.5)
            st["qd2", n] = qd2 = (z * jnp.exp(st["b"][:, cols])).astype(BF16)
            for hd, half in heads_of(n):
                st["qd", hd] = qd2[:, half]

        def key(n):
            z = _dot(st["h"], win_ref[:, OFF_K + n * MXU_COLS:OFF_K + (n + 1) * MXU_COLS])
            for hd, half in heads_of(n):
                bT = st["bT"][hd]
                kT = z[:, half].T
                st["kiT", hd] = (kT * jnp.exp(-bT)).astype(BF16)
                ke, dec = [], []
                for j in range(cpb):
                    bl = bT[:, (j + 1) * c - 1:(j + 1) * c]
                    in_chunk = (ci // c) == j
                    ke.append(jnp.where(in_chunk, kT * jnp.exp(bl - bT), 0.0).astype(BF16))
                    dec.append(jnp.exp(bl))
                st["ke", hd], st["dec", hd] = ke, dec

        def unit(name, off, n, post):
            def f():
                st[name, n] = post(_dot(st["h"], win_ref[:, off + n * MXU_COLS:off + (n + 1) * MXU_COLS]))
            return f

        ident = lambda z: z
        to_bf16 = lambda z: z.astype(BF16)
        vals = [unit("v", OFF_V, n, to_bf16) for n in range(D_V // MXU_COLS)]
        outg = [unit("r", OFF_R, n, ident) for n in range(D_V // MXU_COLS)]
        return st, ([low_rank, vals[0], log_decay, vals[1], cum_decay, vals[2]]
                    + [functools.partial(query, 0), functools.partial(key, 0), functools.partial(query, 1),
                       functools.partial(key, 1), vals[3]] + outg)

    def mix_steps(rb, st):
        res = {}

        def scores(n):
            ha, hb = 2 * n, 2 * n + 1
            zero = jnp.zeros((GLA_DK, nb), BF16)
            keys = jnp.concatenate([jnp.concatenate([st["kiT", ha], zero], axis=1),
                                    jnp.concatenate([zero, st["kiT", hb]], axis=1)], axis=0)
            raw = _dot(st["qd2", n], keys)
            res["araw", ha], res["araw", hb] = raw[:, :nb], raw[:, nb:]

        def values(hd):
            att = jnp.where(causal, res["araw", hd], 0.0).astype(BF16)
            res[hd] = _dot(jnp.concatenate([att] + st["ke", hd], axis=0), st["v", hd])

        def recur(j, hd):
            vc = slice(hd * GLA_DV, (hd + 1) * GLA_DV)
            seg = (rb + j * c) // L
            cr = slice(j * c, (j + 1) * c)
            Sh = S[seg, hd]
            o = _dot(st["qd", hd][cr], Sh.astype(BF16)) + res[hd][cr]
            S[seg, hd] = st["dec", hd][j] * Sh + res[hd][(1 + j) * nb:(2 + j) * nb]
            on = o * lax.rsqrt(jnp.mean(o * o, axis=-1, keepdims=True) + EPS) * gon_ref[:, vc]
            og[rb + j * c:rb + (j + 1) * c, vc] = (on * _silu(st["r", hd][cr])).astype(BF16)

        heads = range(GLA_HEADS)
        return ([functools.partial(scores, n) for n in range(GLA_HEADS // 2)]
                + [functools.partial(values, hd) for hd in heads]
                + [functools.partial(recur, j, hd) for j in range(cpb) for hd in heads])

    def step(w, r):
        def out_steps(rb, st):
            return _out_steps(og, wout_ref, gpost_ref, x_ref, x1, w, slice(rb, rb + nb))

        mixer = _mixer_schedule(list(range(0, t.rows, nb)), project_steps, mix_steps, out_steps)
        ffn = _ffn_steps(t, x1, r, (fgpre_ref, wg_ref, wu_ref, wd_ref, fgpost_ref), xo_ref, act)
        for f in (sum(mixer, []) + ffn if t.single_step else _merge_filler(mixer, ffn)):
            f()

    if t.single_step:
        step(0, 0)
    else:
        step(g % 2, 1 - g % 2)

    @pl.when(g <= last)
    def _():
        sto_ref[...] = S[...]


class LayerOf(NamedTuple):
    stack: jax.Array
    layer: int


def _const_spec(a):
    if isinstance(a, LayerOf):
        shape = a.stack.shape[1:]
        return pl.BlockSpec((None,) + shape, lambda g: (a.layer,) + (0,) * len(shape), pipeline_mode=pl.Buffered(1))
    nd = a.ndim
    return pl.BlockSpec(a.shape, lambda g: (0,) * nd, pipeline_mode=pl.Buffered(1))


def _operand(a):
    return a.stack if isinstance(a, LayerOf) else a


def _no_carry_in(body):
    def wrapped(x_ref, *refs):
        return body(x_ref, None, *refs)
    return wrapped


def _cur_rows(t, width):
    last = t.n_row_tiles - 1
    return pl.BlockSpec((t.rows, width), lambda g: (jnp.minimum(g, last), 0))


def _prev_rows(t, width):
    return pl.BlockSpec((t.rows, width), lambda g: (jnp.maximum(g - 1, 0), 0))


def _cur_seq(t, shape):
    nd = len(shape)
    last = t.n_row_tiles - 1
    return pl.BlockSpec((t.nseg,) + shape, lambda g: (jnp.minimum(g, last) // t.n_tiles,) + (0,) * nd)


def _params():
    return pltpu.CompilerParams(dimension_semantics=("arbitrary",), vmem_limit_bytes=V7X_VMEM_LIMIT_BYTES)


def _ffn_consts(f):
    return (f["gpre"], f["wg"], f["wu"], f["wd"], f["gpost"])


def _even_call(t, emit_v, x, hist, w, f):
    n_rows = x.shape[0]
    consts = (w["gpre"], w["win"], w["wpool"], w["pscale"], w["lng"], w["lnb"], w["ws"], w["bst"], w["wout"],
              w["gpost"]) + _ffn_consts(f)
    c = min(t.seg_len, BLOCK_ROWS)
    hist_spec = _cur_seq(t, (HIST_ROWS, D_POOL))
    out_shape = [jax.ShapeDtypeStruct((n_rows, D_MODEL), F32),
                 jax.ShapeDtypeStruct((t.n_batch * t.nseg, HIST_ROWS, D_POOL), F32)]
    out_specs = [_prev_rows(t, D_MODEL), hist_spec]
    if emit_v:
        out_shape.append(jax.ShapeDtypeStruct((n_rows, D_SG), F32))
        out_specs.append(_cur_rows(t, D_SG))
    body = functools.partial(_even_body, t, emit_v)
    carried = [] if hist is None else [hist]
    return pl.pallas_call(
        _no_carry_in(body) if hist is None else body,
        grid=(t.n_steps,),
        in_specs=[_cur_rows(t, D_MODEL)] + [hist_spec] * len(carried) + [_const_spec(a) for a in consts],
        out_specs=out_specs,
        out_shape=out_shape,
        scratch_shapes=[pltpu.VMEM((2, t.rows, D_MODEL), F32),
                        pltpu.VMEM((t.rows, D_FF), BF16),
                        pltpu.VMEM((t.nseg, HIST_ROWS + t.seg_len, D_POOL), F32),
                        pltpu.VMEM((t.rows, D_POOL + D_SG), BF16),
                        pltpu.VMEM((SG_HEADS, c, c), BF16),
                        pltpu.VMEM((len(POOL_WINDOWS) // 2, MXU_COLS, MXU_COLS), BF16)],
        compiler_params=_params(),
        name="even_layer_%d" % t.seg_len,
    )(x, *carried, *[_operand(a) for a in consts])


def _gla_call(t, x, state, w, f):
    n_rows = x.shape[0]
    consts = (w["gpre"], w["win"], w["wg2"], w["bg"], w["gon"], w["wout"],
              w["gpost"]) + _ffn_consts(f)
    st_spec = _cur_seq(t, (GLA_HEADS, GLA_DK, GLA_DV))
    body = functools.partial(_gla_body, t)
    carried = [] if state is None else [state]
    return pl.pallas_call(
        _no_carry_in(body) if state is None else body,
        grid=(t.n_steps,),
        in_specs=[_cur_rows(t, D_MODEL)] + [st_spec] * len(carried) + [_const_spec(a) for a in consts],
        out_specs=[_prev_rows(t, D_MODEL), st_spec],
        out_shape=[jax.ShapeDtypeStruct((n_rows, D_MODEL), F32),
                   jax.ShapeDtypeStruct((t.n_batch * t.nseg, GLA_HEADS, GLA_DK, GLA_DV), F32)],
        scratch_shapes=[pltpu.VMEM((2, t.rows, D_MODEL), F32),
                        pltpu.VMEM((t.rows, D_FF), BF16),
                        pltpu.VMEM((t.nseg, GLA_HEADS, GLA_DK, GLA_DV), F32),
                        pltpu.VMEM((t.rows, D_V), BF16)],
        compiler_params=_params(),
        name="gla_layer_%d" % t.seg_len,
    )(x, *carried, *[_operand(a) for a in consts])


def _row(g):
    return g.reshape(1, -1)


def _trunk(t, emit_v, x, hist, state, we, wo, wf0, wf1):
    x, hist_out, *v_out = _even_call(t, emit_v, x, hist, we, wf0)
    x, state_out = _gla_call(t, x, state, wo, wf1)
    return x, hist_out, (v_out[0] if emit_v else None), state_out


def kernel(x_prompt, x_sample, state_pool, state_gla, pre_mix_g, post_mix_g, pre_ffn_g, post_ffn_g, w_in_even,
           w_pool, pool_scale, ln_v_g, ln_v_b, w_s, b_s, w_out_even, w_in_odd, w_gate2, b_gate, g_onorm,
           w_out_odd, w_ffn_gate, w_ffn_up, w_ffn_down):
    B, SEQ, _ = x_prompt.shape
    DB, DSEQ, _ = x_sample.shape

    we = dict(gpre=_row(pre_mix_g[0]), win=w_in_even[0].astype(BF16), wpool=w_pool[0].astype(BF16),
              pscale=_row(pool_scale[0]), lng=_row(ln_v_g[0]), lnb=_row(ln_v_b[0]), ws=w_s[0],
              bst=jnp.transpose(b_s[0]), wout=w_out_even[0].astype(BF16), gpost=_row(post_mix_g[0]))
    wo = dict(gpre=_row(pre_mix_g[1]), win=w_in_odd[0].astype(BF16), wg2=w_gate2[0].astype(BF16), bg=_row(b_gate[0]),
              gon=_row(g_onorm[0]), wout=w_out_odd[0].astype(BF16), gpost=_row(post_mix_g[1]))

    wg_all, wu_all, wd_all = w_ffn_gate.astype(BF16), w_ffn_up.astype(BF16), w_ffn_down.astype(BF16)

    def ffn_weights(l):
        return dict(gpre=_row(pre_ffn_g[l]), wg=LayerOf(wg_all, l), wu=LayerOf(wu_all, l), wd=LayerOf(wd_all, l),
                    gpost=_row(post_ffn_g[l]))

    wf0, wf1 = ffn_weights(0), ffn_weights(1)

    tp = Tiling(nseg=1, seg_len=PROMPT_TILE_ROWS, start=0, n_batch=B, n_tiles=SEQ // PROMPT_TILE_ROWS)
    yp, hp, _, sp = _trunk(tp, False, x_prompt.reshape(B * SEQ, D_MODEL), None, None, we, wo, wf0, wf1)

    ts = Tiling(nseg=DB, seg_len=DSEQ, start=PAST_LEN, n_batch=1, n_tiles=1)
    hist_s = jnp.pad(state_pool[0], ((0, 0), (HIST_ROWS - POOL_HIST, 0), (0, 0)))
    ys, hs, vs, ss = _trunk(ts, True, x_sample.reshape(DB * DSEQ, D_MODEL), hist_s, state_gla[0], we, wo, wf0, wf1)

    drop = HIST_ROWS - POOL_HIST
    return (yp.reshape(B, SEQ, D_MODEL), ys.reshape(DB, DSEQ, D_MODEL),
            hp[None, :, drop:], hs[None, :, drop:], vs.reshape(1, DB, DSEQ, D_SG), sp[None], ss[None])
```

```python
import functools
from typing import NamedTuple

import jax
import jax.numpy as jnp
from jax import lax
from jax.experimental import pallas as pl
from jax.experimental.pallas import tpu as pltpu

D_MODEL = 1024
PAST_LEN = 1024
CHUNK = 64
POOL_WINDOWS = (2, 4, 8, 16)
D_POOL = 512
POOL_GROUP = 128
POOL_HIST = 15
HIST_ROWS = 16
SG_CHUNK = 128
D_SG = 512
SG_HEADS = 4
SG_HEAD = 128
D_IN_EVEN = D_POOL + 2 * D_SG
GLA_HEADS = 4
GLA_DK = 128
GLA_DV = 256
GLA_GATE_RANK = 16
GLA_TAU = 16.0
D_QK = 512
D_V = 1024
OFF_Q, OFF_K, OFF_V, OFF_R, OFF_GLR = 0, D_QK, 2 * D_QK, 2 * D_QK + D_V, 2 * D_QK + 2 * D_V
D_FF = 2816
EPS = 1e-6
EVEN_BLOCK_ROWS = 256
GLA_BLOCK_ROWS = 128
FFN_BLOCK_ROWS = 256
PROMPT_TILE_ROWS = 512
MXU_COLS = 256

F32 = jnp.float32
BF16 = jnp.bfloat16
V7X_VMEM_LIMIT_BYTES = 56 * 1024 * 1024


class Tiling(NamedTuple):
    nseg: int
    seg_len: int
    start: int
    n_batch: int
    n_tiles: int

    @property
    def rows(self):
        return self.nseg * self.seg_len

    @property
    def n_row_tiles(self):
        return self.n_batch * self.n_tiles

    @property
    def single_step(self):
        return self.n_row_tiles == 1

    @property
    def n_steps(self):
        return 1 if self.single_step else self.n_row_tiles + 1


def _rms(x, g):
    ms = jnp.mean(x * x, axis=-1, keepdims=True)
    return x * lax.rsqrt(ms + EPS) * g


def _gelu(x):
    c = 0.7978845608028654
    return x * (0.5 * (1.0 + jnp.tanh(c * (x + 0.044715 * (x * x * x)))))


def _silu(x):
    return x / (1.0 + jnp.exp(-x))


def _dot(a, b):
    return jnp.dot(a, b, preferred_element_type=F32)


def _interleave(a, b):
    if len(a) < len(b):
        a, b = b, a
    out, nxt = [], 0
    for n, f in enumerate(a):
        out.append(f)
        while nxt < len(b) and (nxt + 1) * len(a) <= (n + 1) * len(b):
            out.append(b[nxt])
            nxt += 1
    return out + b[nxt:]


def _mixer_schedule(blocks, project_steps, mix_steps, out_steps):
    n = len(blocks)
    states = [None] * n
    states[0], first = project_steps(blocks[0])
    segments = [first]
    for i in range(n):
        free = []
        if i + 1 < n:
            states[i + 1], free = project_steps(blocks[i + 1])
        if i > 0:
            free = _interleave(free, out_steps(blocks[i - 1], states[i - 1]))
        segments.append(_interleave(mix_steps(blocks[i], states[i]), free))
    return segments + [out_steps(blocks[n - 1], states[n - 1])]


def _merge_filler(segments, filler):
    weights = [1.0] * (len(segments) - 2) + [2.0, 0.0]
    if len(segments) == 3:
        weights = [1.0, 2.0, 0.0]
    total = sum(weights)
    order, used, acc = [], 0, 0.0
    for seg, wgt in zip(segments, weights):
        acc += wgt
        upto = len(filler) if acc >= total else int(round(len(filler) * acc / total))
        order += _interleave(seg, filler[used:upto])
        used = upto
    return order


def _out_steps(y_in, wout_ref, gpost_ref, x_ref, x1, w, rows):
    ys = []

    def unit(n):
        ys.append(_dot(y_in[rows, :], wout_ref[:, n * MXU_COLS:(n + 1) * MXU_COLS]))

    def finish():
        x1[w, rows, :] = x_ref[rows, :] + _rms(jnp.concatenate(ys, axis=-1), gpost_ref[...])

    return [functools.partial(unit, n) for n in range(D_MODEL // MXU_COLS)] + [finish]


def _ffn_steps(t, x1, r, fw, xo_ref, act):
    gpre_ref, wg_ref, wu_ref, wd_ref, gpost_ref = fw
    nb = min(FFN_BLOCK_ROWS, t.rows)
    blocks = [slice(rb, rb + nb) for rb in range(0, t.rows, nb)]
    hs, ys = {}, {}

    def norm_in(i):
        hs[i] = _rms(x1[r, blocks[i], :], gpre_ref[...]).astype(BF16)

    def up(i, j):
        cols = slice(j * MXU_COLS, (j + 1) * MXU_COLS)
        act[blocks[i], cols] = (_silu(_dot(hs[i], wg_ref[:, cols])) * _dot(hs[i], wu_ref[:, cols])).astype(BF16)

    def down(i, n):
        ys.setdefault(i, []).append(_dot(act[blocks[i], :], wd_ref[:, n * MXU_COLS:(n + 1) * MXU_COLS]))

    def finish(i):
        xo_ref[blocks[i], :] = x1[r, blocks[i], :] + _rms(jnp.concatenate(ys[i], axis=-1), gpost_ref[...])

    ups = [[functools.partial(norm_in, i)] + [functools.partial(up, i, j) for j in range(D_FF // MXU_COLS)]
           for i in range(len(blocks))]
    downs = [[functools.partial(down, i, n) for n in range(D_MODEL // MXU_COLS)] + [functools.partial(finish, i)]
             for i in range(len(blocks))]
    return sum(ups, []) + sum(downs, [])


def _even_body(t, emit_v, x_ref, hist_ref, gpre_ref, win_ref, wpool_ref, pscale_ref, lng_ref, lnb_ref, ws_ref,
               bst_ref, wout_ref, gpost_ref, fgpre_ref, wg_ref, wu_ref, wd_ref, fgpost_ref, xo_ref, histo_ref,
               *rest):
    if emit_v:
        vo_ref, x1, act, abuf, ycat, wmask, wpool2 = rest
    else:
        x1, act, abuf, ycat, wmask, wpool2 = rest
    g = pl.program_id(0)
    last = t.n_row_tiles - 1
    tpos = jnp.minimum(g, last) % t.n_tiles
    L = t.seg_len
    nb = min(EVEN_BLOCK_ROWS, t.rows)
    c = min(L, SG_CHUNK)
    pair_heads = c == SG_CHUNK

    @pl.when(g == 0)
    def _():
        x1[1] = jnp.zeros(x1.shape[1:], F32)
        tril = (lax.broadcasted_iota(jnp.int32, (c, c), 0) >= lax.broadcasted_iota(jnp.int32, (c, c), 1))
        for hd in range(SG_HEADS):
            wmask[hd] = jnp.where(tril, ws_ref[hd, :c, :c], 0.0).astype(BF16)
        wpool2[...] = jnp.zeros(wpool2.shape, BF16)
        for gi in range(len(POOL_WINDOWS)):
            d = slice((gi % 2) * POOL_GROUP, (gi % 2 + 1) * POOL_GROUP)
            wpool2[gi // 2, d, d] = wpool_ref[gi]

    @pl.when(tpos == 0)
    def _():
        abuf[:, 0:HIST_ROWS, :] = jnp.zeros((t.nseg, HIST_ROWS, D_POOL), F32) if hist_ref is None else hist_ref[...]

    biases = [bst_ref[:c, hd:hd + 1] for hd in range(SG_HEADS)]
    iota_c = lax.broadcasted_iota(jnp.int32, (c, 1), 0)

    def pieces(rb):
        return [(p0 // L, p0 % L, slice(p0, p0 + c)) for p0 in range(rb, rb + nb, c)]

    def project_steps(rb):
        st = {"z": {}}
        rows = slice(rb, rb + nb)

        def unit(n):
            if n == 0:
                st["h"] = _rms(x_ref[rows, :], gpre_ref[...]).astype(BF16)
            st["z"][n] = _dot(st["h"], win_ref[:, n * MXU_COLS:(n + 1) * MXU_COLS])

        def pool_rows():
            st["a"] = a = jnp.concatenate([st["z"][0], st["z"][1]], axis=-1)
            for seg, off, pr in pieces(rb):
                abuf[seg, HIST_ROWS + off:HIST_ROWS + off + c, :] = a[pr.start - rb:pr.stop - rb]

        def gate_u():
            st["u"] = _gelu(jnp.concatenate([st["z"][2], st["z"][3]], axis=-1))

        def gate_v():
            gv = _gelu(jnp.concatenate([st["z"][4], st["z"][5]], axis=-1))
            mu = jnp.mean(gv, axis=-1, keepdims=True)
            dv = gv - mu
            var = jnp.mean(dv * dv, axis=-1, keepdims=True)
            vv = dv * lax.rsqrt(var + EPS) * lng_ref[...] + lnb_ref[...]
            st["vv"] = vv.astype(BF16)
            if emit_v:
                vo_ref[rows, :] = vv

        u = [functools.partial(unit, n) for n in range(D_IN_EVEN // MXU_COLS)]
        return st, [u[0], u[1], pool_rows, u[2], u[3], gate_u, u[4], u[5], gate_v]

    def mix_steps(rb, st):
        def window_mean(seg, off, pr, gi):
            win = POOL_WINDOWS[gi]
            cols = slice(gi * POOL_GROUP, (gi + 1) * POOL_GROUP)
            base = HIST_ROWS + off
            a = st["a"][pr.start - rb:pr.stop - rb, cols]
            acc = a
            for kk in range(1, win):
                acc = acc + abuf[seg, base - kk:base - kk + c, cols]
            cnt = jnp.minimum(win, t.start + tpos * L + off + iota_c + 1).astype(F32)
            return (acc / cnt - a).astype(BF16)

        def pool(seg, off, pr, p):
            cols = slice(2 * p * POOL_GROUP, (2 * p + 2) * POOL_GROUP)
            pooled = jnp.concatenate([window_mean(seg, off, pr, 2 * p), window_mean(seg, off, pr, 2 * p + 1)], axis=1)
            ya = _dot(pooled, wpool2[p]) * pscale_ref[:, cols]
            ycat[pr, cols] = ya.astype(BF16)

        def gate(pr, hd):
            cols = slice(hd * SG_HEAD, (hd + 1) * SG_HEAD)
            br = slice(pr.start - rb, pr.stop - rb)
            mixed = _dot(wmask[hd], st["vv"][br, cols]) + biases[hd]
            yb = st["u"][br, cols] * mixed
            ycat[pr, D_POOL + hd * SG_HEAD:D_POOL + (hd + 1) * SG_HEAD] = yb.astype(BF16)

        def gate_pair(pr, p):
            cols = slice(2 * p * SG_HEAD, (2 * p + 2) * SG_HEAD)
            br = slice(pr.start - rb, pr.stop - rb)
            va = st["vv"][br, 2 * p * SG_HEAD:(2 * p + 1) * SG_HEAD]
            vb = st["vv"][br, (2 * p + 1) * SG_HEAD:(2 * p + 2) * SG_HEAD]
            zero = jnp.zeros_like(va)
            vals = jnp.concatenate([jnp.concatenate([va, zero], axis=1), jnp.concatenate([zero, vb], axis=1)], axis=0)
            wpair = jnp.concatenate([wmask[2 * p], wmask[2 * p + 1]], axis=1)
            bias = jnp.concatenate([jnp.broadcast_to(biases[2 * p], (c, SG_HEAD)),
                                    jnp.broadcast_to(biases[2 * p + 1], (c, SG_HEAD))], axis=1)
            yb = st["u"][br, cols] * (_dot(wpair, vals) + bias)
            ycat[pr, D_POOL + cols.start:D_POOL + cols.stop] = yb.astype(BF16)

        steps = []
        for seg, off, pr in pieces(rb):
            steps += [functools.partial(pool, seg, off, pr, p) for p in range(len(POOL_WINDOWS) // 2)]
            if pair_heads:
                steps += [functools.partial(gate_pair, pr, p) for p in range(SG_HEADS // 2)]
            else:
                steps += [functools.partial(gate, pr, hd) for hd in range(SG_HEADS)]
        return steps

    def step(w, r):
        def out_steps(rb, st):
            return _out_steps(ycat, wout_ref, gpost_ref, x_ref, x1, w, slice(rb, rb + nb))

        mixer = _mixer_schedule(list(range(0, t.rows, nb)), project_steps, mix_steps, out_steps)
        ffn = _ffn_steps(t, x1, r, (fgpre_ref, wg_ref, wu_ref, wd_ref, fgpost_ref), xo_ref, act)
        for f in (sum(mixer, []) + ffn if t.single_step else _merge_filler(mixer, ffn)):
            f()
        abuf[:, 0:HIST_ROWS, :] = abuf[:, L:L + HIST_ROWS, :]

    if t.single_step:
        step(0, 0)
    else:
        step(g % 2, 1 - g % 2)

    @pl.when(g <= last)
    def _():
        histo_ref[...] = abuf[:, 0:HIST_ROWS, :]


def _gla_body(t, x_ref, st_ref, gpre_ref, win_ref, wg2_ref, bg_ref, gon_ref,
              wout_ref, gpost_ref, fgpre_ref, wg_ref, wu_ref, wd_ref, fgpost_ref, xo_ref, sto_ref,
              x1, act, S, og):
    g = pl.program_id(0)
    last = t.n_row_tiles - 1
    tpos = jnp.minimum(g, last) % t.n_tiles
    L = t.seg_len
    nb = min(GLA_BLOCK_ROWS, t.rows)
    c = min(CHUNK, L)
    cpb = nb // c

    @pl.when(g == 0)
    def _():
        x1[1] = jnp.zeros(x1.shape[1:], F32)

    @pl.when(tpos == 0)
    def _():
        S[...] = jnp.zeros(S.shape, F32) if st_ref is None else st_ref[...]

    ri = lax.broadcasted_iota(jnp.int32, (nb, nb), 0)
    ci = lax.broadcasted_iota(jnp.int32, (nb, nb), 1)
    causal = ((ri // c) == (ci // c)) & (ri >= ci)
    tri = jnp.where(causal, 1.0, 0.0).astype(BF16)

    def project_steps(rb):
        st = {}
        rows = slice(rb, rb + nb)

        def low_rank():
            st["h"] = _rms(x_ref[rows, :], gpre_ref[...]).astype(BF16)
            st["glr"] = _dot(st["h"], win_ref[:, OFF_GLR:OFF_GLR + GLA_GATE_RANK])

        def log_decay():
            gate = _dot(st["glr"].astype(BF16), wg2_ref[...]) + bg_ref[...]
            la = (jnp.minimum(gate, 0.0) - jnp.log(1.0 + jnp.exp(-jnp.abs(gate)))) * (1.0 / GLA_TAU)
            la_hi = la.astype(BF16)
            rem = la - la_hi.astype(F32)
            la_mid = rem.astype(BF16)
            st["la"] = (la_hi, la_mid, (rem - la_mid.astype(F32)).astype(BF16))

        def cum_decay():
            la_hi, la_mid, la_lo = st["la"]
            st["b"] = b = _dot(tri, la_hi) + _dot(tri, la_mid) + _dot(tri, la_lo)
            st["bT"] = [b[:, hd * GLA_DK:(hd + 1) * GLA_DK].T for hd in range(GLA_HEADS)]

        def heads_of(n):
            return [(2 * n + e, slice(e * GLA_DK, (e + 1) * GLA_DK)) for e in range(MXU_COLS // GLA_DK)]

        def query(n):
            cols = slice(n * MXU_COLS, (n + 1) * MXU_COLS)
            z = _dot(st["h"], win_ref[:, OFF_Q + cols.start:OFF_Q + cols.stop]) * (GLA_DK ** -0.5)
            st["qd2", n] = qd2 = (z * jnp.exp(st["b"][:, cols])).astype(BF16)
            for hd, half in heads_of(n):
                st["qd", hd] = qd2[:, half]

        def key(n):
            z = _dot(st["h"], win_ref[:, OFF_K + n * MXU_COLS:OFF_K + (n + 1) * MXU_COLS])
            for hd, half in heads_of(n):
                bT = st["bT"][hd]
                kT = z[:, half].T
                st["kiT", hd] = (kT * jnp.exp(-bT)).astype(BF16)
                ke, dec = [], []
                for j in range(cpb):
                    bl = bT[:, (j + 1) * c - 1:(j + 1) * c]
                    in_chunk = (lax.broadcasted_iota(jnp.int32, (GLA_DK, nb), 1) // c) == j
                    ke.append(jnp.where(in_chunk, kT * jnp.exp(bl - bT), 0.0).astype(BF16))
                    dec.append(jnp.exp(bl))
                st["ke", hd], st["dec", hd] = ke, dec

        def unit(name, off, n, post):
            def f():
                st[name, n] = post(_dot(st["h"], win_ref[:, off + n * MXU_COLS:off + (n + 1) * MXU_COLS]))
            return f

        ident = lambda z: z
        to_bf16 = lambda z: z.astype(BF16)
        vals = [unit("v", OFF_V, n, to_bf16) for n in range(D_V // MXU_COLS)]
        outg = [unit("r", OFF_R, n, ident) for n in range(D_V // MXU_COLS)]
        return st, ([low_rank, vals[0], log_decay, vals[1], cum_decay, vals[2]]
                    + [functools.partial(query, 0), functools.partial(key, 0), functools.partial(query, 1),
                       functools.partial(key, 1), vals[3]] + outg)

    def mix_steps(rb, st):
        res = {}

        def scores(n):
            ha, hb = 2 * n, 2 * n + 1
            zero = jnp.zeros((GLA_DK, nb), BF16)
            keys = jnp.concatenate([jnp.concatenate([st["kiT", ha], zero], axis=1),
                                    jnp.concatenate([zero, st["kiT", hb]], axis=1)], axis=0)
            raw = _dot(st["qd2", n], keys)
            res["araw", ha], res["araw", hb] = raw[:, :nb], raw[:, nb:]

        def values(hd):
            att = jnp.where(causal, res["araw", hd], 0.0).astype(BF16)
            res[hd] = _dot(jnp.concatenate([att] + st["ke", hd], axis=0), st["v", hd])

        def recur(j, hd):
            vc = slice(hd * GLA_DV, (hd + 1) * GLA_DV)
            seg = (rb + j * c) // L
            cr = slice(j * c, (j + 1) * c)
            Sh = S[seg, hd]
            o = _dot(st["qd", hd][cr], Sh.astype(BF16)) + res[hd][cr]
            S[seg, hd] = st["dec", hd][j] * Sh + res[hd][nb + j * GLA_DK:nb + (j + 1) * GLA_DK]
            on = o * lax.rsqrt(jnp.mean(o * o, axis=-1, keepdims=True) + EPS) * gon_ref[:, vc]
            og[rb + j * c:rb + (j + 1) * c, vc] = (on * _silu(st["r", hd][cr])).astype(BF16)

        heads = range(GLA_HEADS)
        return ([functools.partial(scores, n) for n in range(GLA_HEADS // 2)]
                + [functools.partial(values, hd) for hd in heads]
                + [functools.partial(recur, j, hd) for j in range(cpb) for hd in heads])

    def step(w, r):
        def out_steps(rb, st):
            return _out_steps(og, wout_ref, gpost_ref, x_ref, x1, w, slice(rb, rb + nb))

        mixer = _mixer_schedule(list(range(0, t.rows, nb)), project_steps, mix_steps, out_steps)
        ffn = _ffn_steps(t, x1, r, (fgpre_ref, wg_ref, wu_ref, wd_ref, fgpost_ref), xo_ref, act)
        for f in (sum(mixer, []) + ffn if t.single_step else _merge_filler(mixer, ffn)):
            f()

    if t.single_step:
        step(0, 0)
    else:
        step(g % 2, 1 - g % 2)

    @pl.when(g <= last)
    def _():
        sto_ref[...] = S[...]


class LayerOf(NamedTuple):
    stack: jax.Array
    layer: int


def _const_spec(a):
    if isinstance(a, LayerOf):
        shape = a.stack.shape[1:]
        return pl.BlockSpec((None,) + shape, lambda g: (a.layer,) + (0,) * len(shape), pipeline_mode=pl.Buffered(1))
    nd = a.ndim
    return pl.BlockSpec(a.shape, lambda g: (0,) * nd, pipeline_mode=pl.Buffered(1))


def _operand(a):
    return a.stack if isinstance(a, LayerOf) else a


def _no_carry_in(body):
    def wrapped(x_ref, *refs):
        return body(x_ref, None, *refs)
    return wrapped


def _cur_rows(t, width):
    last = t.n_row_tiles - 1
    return pl.BlockSpec((t.rows, width), lambda g: (jnp.minimum(g, last), 0))


def _prev_rows(t, width):
    return pl.BlockSpec((t.rows, width), lambda g: (jnp.maximum(g - 1, 0), 0))


def _cur_seq(t, shape):
    nd = len(shape)
    last = t.n_row_tiles - 1
    return pl.BlockSpec((t.nseg,) + shape, lambda g: (jnp.minimum(g, last) // t.n_tiles,) + (0,) * nd)


def _params():
    return pltpu.CompilerParams(dimension_semantics=("arbitrary",), vmem_limit_bytes=V7X_VMEM_LIMIT_BYTES)


def _ffn_consts(f):
    return (f["gpre"], f["wg"], f["wu"], f["wd"], f["gpost"])


def _even_call(t, emit_v, x, hist, w, f):
    n_rows = x.shape[0]
    consts = (w["gpre"], w["win"], w["wpool"], w["pscale"], w["lng"], w["lnb"], w["ws"], w["bst"], w["wout"],
              w["gpost"]) + _ffn_consts(f)
    c = min(t.seg_len, SG_CHUNK)
    hist_spec = _cur_seq(t, (HIST_ROWS, D_POOL))
    out_shape = [jax.ShapeDtypeStruct((n_rows, D_MODEL), F32),
                 jax.ShapeDtypeStruct((t.n_batch * t.nseg, HIST_ROWS, D_POOL), F32)]
    out_specs = [_prev_rows(t, D_MODEL), hist_spec]
    if emit_v:
        out_shape.append(jax.ShapeDtypeStruct((n_rows, D_SG), F32))
        out_specs.append(_cur_rows(t, D_SG))
    body = functools.partial(_even_body, t, emit_v)
    carried = [] if hist is None else [hist]
    return pl.pallas_call(
        _no_carry_in(body) if hist is None else body,
        grid=(t.n_steps,),
        in_specs=[_cur_rows(t, D_MODEL)] + [hist_spec] * len(carried) + [_const_spec(a) for a in consts],
        out_specs=out_specs,
        out_shape=out_shape,
        scratch_shapes=[pltpu.VMEM((2, t.rows, D_MODEL), F32),
                        pltpu.VMEM((t.rows, D_FF), BF16),
                        pltpu.VMEM((t.nseg, HIST_ROWS + t.seg_len, D_POOL), F32),
                        pltpu.VMEM((t.rows, D_POOL + D_SG), BF16),
                        pltpu.VMEM((SG_HEADS, c, c), BF16),
                        pltpu.VMEM((len(POOL_WINDOWS) // 2, MXU_COLS, MXU_COLS), BF16)],
        compiler_params=_params(),
        name="even_layer_%d" % t.seg_len,
    )(x, *carried, *[_operand(a) for a in consts])


def _gla_call(t, x, state, w, f):
    n_rows = x.shape[0]
    consts = (w["gpre"], w["win"], w["wg2"], w["bg"], w["gon"], w["wout"],
              w["gpost"]) + _ffn_consts(f)
    st_spec = _cur_seq(t, (GLA_HEADS, GLA_DK, GLA_DV))
    body = functools.partial(_gla_body, t)
    carried = [] if state is None else [state]
    return pl.pallas_call(
        _no_carry_in(body) if state is None else body,
        grid=(t.n_steps,),
        in_specs=[_cur_rows(t, D_MODEL)] + [st_spec] * len(carried) + [_const_spec(a) for a in consts],
        out_specs=[_prev_rows(t, D_MODEL), st_spec],
        out_shape=[jax.ShapeDtypeStruct((n_rows, D_MODEL), F32),
                   jax.ShapeDtypeStruct((t.n_batch * t.nseg, GLA_HEADS, GLA_DK, GLA_DV), F32)],
        scratch_shapes=[pltpu.VMEM((2, t.rows, D_MODEL), F32),
                        pltpu.VMEM((t.rows, D_FF), BF16),
                        pltpu.VMEM((t.nseg, GLA_HEADS, GLA_DK, GLA_DV), F32),
                        pltpu.VMEM((t.rows, D_V), BF16)],
        compiler_params=_params(),
        name="gla_layer_%d" % t.seg_len,
    )(x, *carried, *[_operand(a) for a in consts])


def _row(g):
    return g.reshape(1, -1)


def _trunk(t, emit_v, x, hist, state, we, wo, wf0, wf1):
    x, hist_out, *v_out = _even_call(t, emit_v, x, hist, we, wf0)
    x, state_out = _gla_call(t, x, state, wo, wf1)
    return x, hist_out, (v_out[0] if emit_v else None), state_out


def kernel(x_prompt, x_sample, state_pool, state_gla, pre_mix_g, post_mix_g, pre_ffn_g, post_ffn_g, w_in_even,
           w_pool, pool_scale, ln_v_g, ln_v_b, w_s, b_s, w_out_even, w_in_odd, w_gate2, b_gate, g_onorm,
           w_out_odd, w_ffn_gate, w_ffn_up, w_ffn_down):
    B, SEQ, _ = x_prompt.shape
    DB, DSEQ, _ = x_sample.shape

    we = dict(gpre=_row(pre_mix_g[0]), win=w_in_even[0].astype(BF16), wpool=w_pool[0].astype(BF16),
              pscale=_row(pool_scale[0]), lng=_row(ln_v_g[0]), lnb=_row(ln_v_b[0]), ws=w_s[0],
              bst=jnp.transpose(b_s[0]), wout=w_out_even[0].astype(BF16), gpost=_row(post_mix_g[0]))
    wo = dict(gpre=_row(pre_mix_g[1]), win=w_in_odd[0].astype(BF16), wg2=w_gate2[0].astype(BF16), bg=_row(b_gate[0]),
              gon=_row(g_onorm[0]), wout=w_out_odd[0].astype(BF16), gpost=_row(post_mix_g[1]))

    wg_all, wu_all, wd_all = w_ffn_gate.astype(BF16), w_ffn_up.astype(BF16), w_ffn_down.astype(BF16)

    def ffn_weights(l):
        return dict(gpre=_row(pre_ffn_g[l]), wg=LayerOf(wg_all, l), wu=LayerOf(wu_all, l), wd=LayerOf(wd_all, l),
                    gpost=_row(post_ffn_g[l]))

    wf0, wf1 = ffn_weights(0), ffn_weights(1)

    tp = Tiling(nseg=1, seg_len=PROMPT_TILE_ROWS, start=0, n_batch=B, n_tiles=SEQ // PROMPT_TILE_ROWS)
    yp, hp, _, sp = _trunk(tp, False, x_prompt.reshape(B * SEQ, D_MODEL), None, None, we, wo, wf0, wf1)

    ts = Tiling(nseg=DB, seg_len=DSEQ, start=PAST_LEN, n_batch=1, n_tiles=1)
    hist_s = jnp.pad(state_pool[0], ((0, 0), (HIST_ROWS - POOL_HIST, 0), (0, 0)))
    ys, hs, vs, ss = _trunk(ts, True, x_sample.reshape(DB * DSEQ, D_MODEL), hist_s, state_gla[0], we, wo, wf0, wf1)

    drop = HIST_ROWS - POOL_HIST
    return (yp.reshape(B, SEQ, D_MODEL), ys.reshape(DB, DSEQ, D_MODEL),
            hp[None, :, drop:], hs[None, :, drop:], vs.reshape(1, DB, DSEQ, D_SG), sp[None], ss[None])
```

```python
import functools
from typing import NamedTuple

import jax
import jax.numpy as jnp
from jax import lax
from jax.experimental import pallas as pl
from jax.experimental.pallas import tpu as pltpu

D_MODEL = 1024
PAST_LEN = 1024
CHUNK = 64
POOL_WINDOWS = (2, 4, 8, 16)
D_POOL = 512
POOL_GROUP = 128
POOL_HIST = 15
HIST_ROWS = 16
SG_CHUNK = 128
D_SG = 512
SG_HEADS = 4
SG_HEAD = 128
D_IN_EVEN = D_POOL + 2 * D_SG
GLA_HEADS = 4
GLA_DK = 128
GLA_DV = 256
GLA_GATE_RANK = 16
GLA_TAU = 16.0
D_QK = 512
D_V = 1024
OFF_Q, OFF_K, OFF_V, OFF_R, OFF_GLR = 0, D_QK, 2 * D_QK, 2 * D_QK + D_V, 2 * D_QK + 2 * D_V
D_FF = 2816
EPS = 1e-6
EVEN_BLOCK_ROWS = 256
GLA_BLOCK_ROWS = 256
FFN_BLOCK_ROWS = 256
PROMPT_TILE_ROWS = 512
MXU_COLS = 256

F32 = jnp.float32
BF16 = jnp.bfloat16
V7X_VMEM_LIMIT_BYTES = 56 * 1024 * 1024


class Tiling(NamedTuple):
    nseg: int
    seg_len: int
    start: int
    n_batch: int
    n_tiles: int

    @property
    def rows(self):
        return self.nseg * self.seg_len

    @property
    def n_row_tiles(self):
        return self.n_batch * self.n_tiles

    @property
    def single_step(self):
        return self.n_row_tiles == 1

    @property
    def n_steps(self):
        return 1 if self.single_step else self.n_row_tiles + 1


def _rms(x, g):
    ms = jnp.mean(x * x, axis=-1, keepdims=True)
    return x * lax.rsqrt(ms + EPS) * g


def _gelu(x):
    c = 0.7978845608028654
    return x * (0.5 * (1.0 + jnp.tanh(c * (x + 0.044715 * (x * x * x)))))


def _silu(x):
    return x / (1.0 + jnp.exp(-x))


def _dot(a, b):
    return jnp.dot(a, b, preferred_element_type=F32)


def _interleave(a, b):
    if len(a) < len(b):
        a, b = b, a
    out, nxt = [], 0
    for n, f in enumerate(a):
        out.append(f)
        while nxt < len(b) and (nxt + 1) * len(a) <= (n + 1) * len(b):
            out.append(b[nxt])
            nxt += 1
    return out + b[nxt:]


def _mixer_schedule(blocks, project_steps, mix_steps, out_steps):
    n = len(blocks)
    states = [None] * n
    states[0], first = project_steps(blocks[0])
    segments = [first]
    for i in range(n):
        free = []
        if i + 1 < n:
            states[i + 1], free = project_steps(blocks[i + 1])
        if i > 0:
            free = _interleave(free, out_steps(blocks[i - 1], states[i - 1]))
        segments.append(_interleave(mix_steps(blocks[i], states[i]), free))
    return segments + [out_steps(blocks[n - 1], states[n - 1])]


def _merge_filler(segments, filler):
    weights = [1.0] * (len(segments) - 2) + [2.0, 0.0]
    if len(segments) == 3:
        weights = [1.0, 2.0, 0.0]
    total = sum(weights)
    order, used, acc = [], 0, 0.0
    for seg, wgt in zip(segments, weights):
        acc += wgt
        upto = len(filler) if acc >= total else int(round(len(filler) * acc / total))
        order += _interleave(seg, filler[used:upto])
        used = upto
    return order


def _out_steps(y_in, wout_ref, gpost_ref, x_ref, x1, w, rows):
    ys = []

    def unit(n):
        ys.append(_dot(y_in[rows, :], wout_ref[:, n * MXU_COLS:(n + 1) * MXU_COLS]))

    def finish():
        x1[w, rows, :] = x_ref[rows, :] + _rms(jnp.concatenate(ys, axis=-1), gpost_ref[...])

    return [functools.partial(unit, n) for n in range(D_MODEL // MXU_COLS)] + [finish]


def _ffn_steps(t, x1, r, fw, xo_ref, act):
    gpre_ref, wg_ref, wu_ref, wd_ref, gpost_ref = fw
    nb = min(FFN_BLOCK_ROWS, t.rows)
    blocks = [slice(rb, rb + nb) for rb in range(0, t.rows, nb)]
    hs, ys = {}, {}

    def norm_in(i):
        hs[i] = _rms(x1[r, blocks[i], :], gpre_ref[...]).astype(BF16)

    def up(i, j):
        cols = slice(j * MXU_COLS, (j + 1) * MXU_COLS)
        act[blocks[i], cols] = (_silu(_dot(hs[i], wg_ref[:, cols])) * _dot(hs[i], wu_ref[:, cols])).astype(BF16)

    def down(i, n):
        ys.setdefault(i, []).append(_dot(act[blocks[i], :], wd_ref[:, n * MXU_COLS:(n + 1) * MXU_COLS]))

    def finish(i):
        xo_ref[blocks[i], :] = x1[r, blocks[i], :] + _rms(jnp.concatenate(ys[i], axis=-1), gpost_ref[...])

    ups = [[functools.partial(norm_in, i)] + [functools.partial(up, i, j) for j in range(D_FF // MXU_COLS)]
           for i in range(len(blocks))]
    downs = [[functools.partial(down, i, n) for n in range(D_MODEL // MXU_COLS)] + [functools.partial(finish, i)]
             for i in range(len(blocks))]
    return sum(ups, []) + sum(downs, [])


def _even_body(t, emit_v, x_ref, hist_ref, gpre_ref, win_ref, wpool_ref, pscale_ref, lng_ref, lnb_ref, ws_ref,
               bst_ref, wout_ref, gpost_ref, fgpre_ref, wg_ref, wu_ref, wd_ref, fgpost_ref, xo_ref, histo_ref,
               *rest):
    if emit_v:
        vo_ref, x1, act, abuf, ycat, wmask, wpool2 = rest
    else:
        x1, act, abuf, ycat, wmask, wpool2 = rest
    g = pl.program_id(0)
    last = t.n_row_tiles - 1
    tpos = jnp.minimum(g, last) % t.n_tiles
    L = t.seg_len
    nb = min(EVEN_BLOCK_ROWS, t.rows)
    c = min(L, SG_CHUNK)
    pair_heads = c == SG_CHUNK

    @pl.when(g == 0)
    def _():
        x1[1] = jnp.zeros(x1.shape[1:], F32)
        tril = (lax.broadcasted_iota(jnp.int32, (c, c), 0) >= lax.broadcasted_iota(jnp.int32, (c, c), 1))
        for hd in range(SG_HEADS):
            wmask[hd] = jnp.where(tril, ws_ref[hd, :c, :c], 0.0).astype(BF16)
        wpool2[...] = jnp.zeros(wpool2.shape, BF16)
        for gi in range(len(POOL_WINDOWS)):
            d = slice((gi % 2) * POOL_GROUP, (gi % 2 + 1) * POOL_GROUP)
            wpool2[gi // 2, d, d] = wpool_ref[gi]

    @pl.when(tpos == 0)
    def _():
        abuf[:, 0:HIST_ROWS, :] = jnp.zeros((t.nseg, HIST_ROWS, D_POOL), F32) if hist_ref is None else hist_ref[...]

    biases = [bst_ref[:c, hd:hd + 1] for hd in range(SG_HEADS)]
    iota_c = lax.broadcasted_iota(jnp.int32, (c, 1), 0)

    def pieces(rb):
        return [(p0 // L, p0 % L, slice(p0, p0 + c)) for p0 in range(rb, rb + nb, c)]

    def project_steps(rb):
        st = {"z": {}}
        rows = slice(rb, rb + nb)

        def unit(n):
            if n == 0:
                st["h"] = _rms(x_ref[rows, :], gpre_ref[...]).astype(BF16)
            st["z"][n] = _dot(st["h"], win_ref[:, n * MXU_COLS:(n + 1) * MXU_COLS])

        def pool_rows():
            st["a"] = a = jnp.concatenate([st["z"][0], st["z"][1]], axis=-1)
            for seg, off, pr in pieces(rb):
                abuf[seg, HIST_ROWS + off:HIST_ROWS + off + c, :] = a[pr.start - rb:pr.stop - rb]

        def gate_u():
            st["u"] = _gelu(jnp.concatenate([st["z"][2], st["z"][3]], axis=-1))

        def gate_v():
            gv = _gelu(jnp.concatenate([st["z"][4], st["z"][5]], axis=-1))
            mu = jnp.mean(gv, axis=-1, keepdims=True)
            dv = gv - mu
            var = jnp.mean(dv * dv, axis=-1, keepdims=True)
            vv = dv * lax.rsqrt(var + EPS) * lng_ref[...] + lnb_ref[...]
            st["vv"] = vv.astype(BF16)
            if emit_v:
                vo_ref[rows, :] = vv

        u = [functools.partial(unit, n) for n in range(D_IN_EVEN // MXU_COLS)]
        return st, [u[0], u[1], pool_rows, u[2], u[3], gate_u, u[4], u[5], gate_v]

    def mix_steps(rb, st):
        def window_mean(seg, off, pr, gi):
            win = POOL_WINDOWS[gi]
            cols = slice(gi * POOL_GROUP, (gi + 1) * POOL_GROUP)
            base = HIST_ROWS + off
            a = st["a"][pr.start - rb:pr.stop - rb, cols]
            acc = a
            for kk in range(1, win):
                acc = acc + abuf[seg, base - kk:base - kk + c, cols]
            cnt = jnp.minimum(win, t.start + tpos * L + off + iota_c + 1).astype(F32)
            return (acc / cnt - a).astype(BF16)

        def pool(seg, off, pr, p):
            cols = slice(2 * p * POOL_GROUP, (2 * p + 2) * POOL_GROUP)
            pooled = jnp.concatenate([window_mean(seg, off, pr, 2 * p), window_mean(seg, off, pr, 2 * p + 1)], axis=1)
            ya = _dot(pooled, wpool2[p]) * pscale_ref[:, cols]
            ycat[pr, cols] = ya.astype(BF16)

        def gate(pr, hd):
            cols = slice(hd * SG_HEAD, (hd + 1) * SG_HEAD)
            br = slice(pr.start - rb, pr.stop - rb)
            mixed = _dot(wmask[hd], st["vv"][br, cols]) + biases[hd]
            yb = st["u"][br, cols] * mixed
            ycat[pr, D_POOL + hd * SG_HEAD:D_POOL + (hd + 1) * SG_HEAD] = yb.astype(BF16)

        def gate_pair(pr, p):
            cols = slice(2 * p * SG_HEAD, (2 * p + 2) * SG_HEAD)
            br = slice(pr.start - rb, pr.stop - rb)
            va = st["vv"][br, 2 * p * SG_HEAD:(2 * p + 1) * SG_HEAD]
            vb = st["vv"][br, (2 * p + 1) * SG_HEAD:(2 * p + 2) * SG_HEAD]
            zero = jnp.zeros_like(va)
            vals = jnp.concatenate([jnp.concatenate([va, zero], axis=1), jnp.concatenate([zero, vb], axis=1)], axis=0)
            wpair = jnp.concatenate([wmask[2 * p], wmask[2 * p + 1]], axis=1)
            bias = jnp.concatenate([jnp.broadcast_to(biases[2 * p], (c, SG_HEAD)),
                                    jnp.broadcast_to(biases[2 * p + 1], (c, SG_HEAD))], axis=1)
            yb = st["u"][br, cols] * (_dot(wpair, vals) + bias)
            ycat[pr, D_POOL + cols.start:D_POOL + cols.stop] = yb.astype(BF16)

        steps = []
        for seg, off, pr in pieces(rb):
            steps += [functools.partial(pool, seg, off, pr, p) for p in range(len(POOL_WINDOWS) // 2)]
            if pair_heads:
                steps += [functools.partial(gate_pair, pr, p) for p in range(SG_HEADS // 2)]
            else:
                steps += [functools.partial(gate, pr, hd) for hd in range(SG_HEADS)]
        return steps

    def step(w, r):
        def out_steps(rb, st):
            return _out_steps(ycat, wout_ref, gpost_ref, x_ref, x1, w, slice(rb, rb + nb))

        mixer = _mixer_schedule(list(range(0, t.rows, nb)), project_steps, mix_steps, out_steps)
        ffn = _ffn_steps(t, x1, r, (fgpre_ref, wg_ref, wu_ref, wd_ref, fgpost_ref), xo_ref, act)
        for f in (sum(mixer, []) + ffn if t.single_step else _merge_filler(mixer, ffn)):
            f()
        abuf[:, 0:HIST_ROWS, :] = abuf[:, L:L + HIST_ROWS, :]

    if t.single_step:
        step(0, 0)
    else:
        step(g % 2, 1 - g % 2)

    @pl.when(g <= last)
    def _():
        histo_ref[...] = abuf[:, 0:HIST_ROWS, :]


def _gla_body(t, x_ref, st_ref, gpre_ref, win_ref, wg2_ref, bg_ref, gon_ref,
              wout_ref, gpost_ref, fgpre_ref, wg_ref, wu_ref, wd_ref, fgpost_ref, xo_ref, sto_ref,
              x1, act, S, og):
    g = pl.program_id(0)
    last = t.n_row_tiles - 1
    tpos = jnp.minimum(g, last) % t.n_tiles
    L = t.seg_len
    nb = min(GLA_BLOCK_ROWS, t.rows)
    c = min(CHUNK, L)
    cpb = nb // c

    @pl.when(g == 0)
    def _():
        x1[1] = jnp.zeros(x1.shape[1:], F32)

    @pl.when(tpos == 0)
    def _():
        S[...] = jnp.zeros(S.shape, F32) if st_ref is None else st_ref[...]

    ri = lax.broadcasted_iota(jnp.int32, (nb, nb), 0)
    ci = lax.broadcasted_iota(jnp.int32, (nb, nb), 1)
    causal = ((ri // c) == (ci // c)) & (ri >= ci)
    tri = jnp.where(causal, 1.0, 0.0).astype(BF16)

    def project_steps(rb):
        st = {}
        rows = slice(rb, rb + nb)

        def low_rank():
            st["h"] = _rms(x_ref[rows, :], gpre_ref[...]).astype(BF16)
            st["glr"] = _dot(st["h"], win_ref[:, OFF_GLR:OFF_GLR + GLA_GATE_RANK])

        def log_decay():
            gate = _dot(st["glr"].astype(BF16), wg2_ref[...]) + bg_ref[...]
            la = (jnp.minimum(gate, 0.0) - jnp.log(1.0 + jnp.exp(-jnp.abs(gate)))) * (1.0 / GLA_TAU)
            la_hi = la.astype(BF16)
            rem = la - la_hi.astype(F32)
            la_mid = rem.astype(BF16)
            st["la"] = (la_hi, la_mid, (rem - la_mid.astype(F32)).astype(BF16))

        def cum_decay():
            la_hi, la_mid, la_lo = st["la"]
            st["b"] = b = _dot(tri, la_hi) + _dot(tri, la_mid) + _dot(tri, la_lo)
            st["bT"] = [b[:, hd * GLA_DK:(hd + 1) * GLA_DK].T for hd in range(GLA_HEADS)]

        def heads_of(n):
            return [(2 * n + e, slice(e * GLA_DK, (e + 1) * GLA_DK)) for e in range(MXU_COLS // GLA_DK)]

        def query(n):
            cols = slice(n * MXU_COLS, (n + 1) * MXU_COLS)
            z = _dot(st["h"], win_ref[:, OFF_Q + cols.start:OFF_Q + cols.stop]) * (GLA_DK ** -0.5)
            st["qd2", n] = qd2 = (z * jnp.exp(st["b"][:, cols])).astype(BF16)
            for hd, half in heads_of(n):
                st["qd", hd] = qd2[:, half]

        def key(n):
            z = _dot(st["h"], win_ref[:, OFF_K + n * MXU_COLS:OFF_K + (n + 1) * MXU_COLS])
            for hd, half in heads_of(n):
                bT = st["bT"][hd]
                kT = z[:, half].T
                st["kiT", hd] = (kT * jnp.exp(-bT)).astype(BF16)
                ke, dec = [], []
                for j in range(cpb):
                    bl = bT[:, (j + 1) * c - 1:(j + 1) * c]
                    in_chunk = (lax.broadcasted_iota(jnp.int32, (GLA_DK, nb), 1) // c) == j
                    ke.append(jnp.where(in_chunk, kT * jnp.exp(bl - bT), 0.0).astype(BF16))
                    dec.append(jnp.exp(bl))
                st["ke", hd], st["dec", hd] = ke, dec

        def unit(name, off, n, post):
            def f():
                st[name, n] = post(_dot(st["h"], win_ref[:, off + n * MXU_COLS:off + (n + 1) * MXU_COLS]))
            return f

        ident = lambda z: z
        to_bf16 = lambda z: z.astype(BF16)
        vals = [unit("v", OFF_V, n, to_bf16) for n in range(D_V // MXU_COLS)]
        outg = [unit("r", OFF_R, n, ident) for n in range(D_V // MXU_COLS)]
        return st, ([low_rank, vals[0], log_decay, vals[1], cum_decay, vals[2]]
                    + [functools.partial(query, 0), functools.partial(key, 0), functools.partial(query, 1),
                       functools.partial(key, 1), vals[3]] + outg)

    def mix_steps(rb, st):
        res = {}

        def scores(n):
            ha, hb = 2 * n, 2 * n + 1
            zero = jnp.zeros((GLA_DK, nb), BF16)
            keys = jnp.concatenate([jnp.concatenate([st["kiT", ha], zero], axis=1),
                                    jnp.concatenate([zero, st["kiT", hb]], axis=1)], axis=0)
            raw = _dot(st["qd2", n], keys)
            res["araw", ha], res["araw", hb] = raw[:, :nb], raw[:, nb:]

        def values(hd):
            att = jnp.where(causal, res["araw", hd], 0.0).astype(BF16)
            res[hd] = _dot(jnp.concatenate([att] + st["ke", hd], axis=0), st["v", hd])

        def recur(j, hd):
            vc = slice(hd * GLA_DV, (hd + 1) * GLA_DV)
            seg = (rb + j * c) // L
            cr = slice(j * c, (j + 1) * c)
            Sh = S[seg, hd]
            o = _dot(st["qd", hd][cr], Sh.astype(BF16)) + res[hd][cr]
            S[seg, hd] = st["dec", hd][j] * Sh + res[hd][nb + j * GLA_DK:nb + (j + 1) * GLA_DK]
            on = o * lax.rsqrt(jnp.mean(o * o, axis=-1, keepdims=True) + EPS) * gon_ref[:, vc]
            og[rb + j * c:rb + (j + 1) * c, vc] = (on * _silu(st["r", hd][cr])).astype(BF16)

        heads = range(GLA_HEADS)
        return ([functools.partial(scores, n) for n in range(GLA_HEADS // 2)]
                + [functools.partial(values, hd) for hd in heads]
                + [functools.partial(recur, j, hd) for j in range(cpb) for hd in heads])

    def step(w, r):
        def out_steps(rb, st):
            return _out_steps(og, wout_ref, gpost_ref, x_ref, x1, w, slice(rb, rb + nb))

        mixer = _mixer_schedule(list(range(0, t.rows, nb)), project_steps, mix_steps, out_steps)
        ffn = _ffn_steps(t, x1, r, (fgpre_ref, wg_ref, wu_ref, wd_ref, fgpost_ref), xo_ref, act)
        for f in (sum(mixer, []) + ffn if t.single_step else _merge_filler(mixer, ffn)):
            f()

    if t.single_step:
        step(0, 0)
    else:
        step(g % 2, 1 - g % 2)

    @pl.when(g <= last)
    def _():
        sto_ref[...] = S[...]


class LayerOf(NamedTuple):
    stack: jax.Array
    layer: int


def _const_spec(a):
    if isinstance(a, LayerOf):
        shape = a.stack.shape[1:]
        return pl.BlockSpec((None,) + shape, lambda g: (a.layer,) + (0,) * len(shape), pipeline_mode=pl.Buffered(1))
    nd = a.ndim
    return pl.BlockSpec(a.shape, lambda g: (0,) * nd, pipeline_mode=pl.Buffered(1))


def _operand(a):
    return a.stack if isinstance(a, LayerOf) else a


def _no_carry_in(body):
    def wrapped(x_ref, *refs):
        return body(x_ref, None, *refs)
    return wrapped


def _cur_rows(t, width):
    last = t.n_row_tiles - 1
    return pl.BlockSpec((t.rows, width), lambda g: (jnp.minimum(g, last), 0))


def _prev_rows(t, width):
    return pl.BlockSpec((t.rows, width), lambda g: (jnp.maximum(g - 1, 0), 0))


def _cur_seq(t, shape):
    nd = len(shape)
    last = t.n_row_tiles - 1
    return pl.BlockSpec((t.nseg,) + shape, lambda g: (jnp.minimum(g, last) // t.n_tiles,) + (0,) * nd)


def _params():
    return pltpu.CompilerParams(dimension_semantics=("arbitrary",), vmem_limit_bytes=V7X_VMEM_LIMIT_BYTES)


def _ffn_consts(f):
    return (f["gpre"], f["wg"], f["wu"], f["wd"], f["gpost"])


def _even_call(t, emit_v, x, hist, w, f):
    n_rows = x.shape[0]
    consts = (w["gpre"], w["win"], w["wpool"], w["pscale"], w["lng"], w["lnb"], w["ws"], w["bst"], w["wout"],
              w["gpost"]) + _ffn_consts(f)
    c = min(t.seg_len, SG_CHUNK)
    hist_spec = _cur_seq(t, (HIST_ROWS, D_POOL))
    out_shape = [jax.ShapeDtypeStruct((n_rows, D_MODEL), F32),
                 jax.ShapeDtypeStruct((t.n_batch * t.nseg, HIST_ROWS, D_POOL), F32)]
    out_specs = [_prev_rows(t, D_MODEL), hist_spec]
    if emit_v:
        out_shape.append(jax.ShapeDtypeStruct((n_rows, D_SG), F32))
        out_specs.append(_cur_rows(t, D_SG))
    body = functools.partial(_even_body, t, emit_v)
    carried = [] if hist is None else [hist]
    return pl.pallas_call(
        _no_carry_in(body) if hist is None else body,
        grid=(t.n_steps,),
        in_specs=[_cur_rows(t, D_MODEL)] + [hist_spec] * len(carried) + [_const_spec(a) for a in consts],
        out_specs=out_specs,
        out_shape=out_shape,
        scratch_shapes=[pltpu.VMEM((2, t.rows, D_MODEL), F32),
                        pltpu.VMEM((t.rows, D_FF), BF16),
                        pltpu.VMEM((t.nseg, HIST_ROWS + t.seg_len, D_POOL), F32),
                        pltpu.VMEM((t.rows, D_POOL + D_SG), BF16),
                        pltpu.VMEM((SG_HEADS, c, c), BF16),
                        pltpu.VMEM((len(POOL_WINDOWS) // 2, MXU_COLS, MXU_COLS), BF16)],
        compiler_params=_params(),
        name="even_layer_%d" % t.seg_len,
    )(x, *carried, *[_operand(a) for a in consts])


def _gla_call(t, x, state, w, f):
    n_rows = x.shape[0]
    consts = (w["gpre"], w["win"], w["wg2"], w["bg"], w["gon"], w["wout"],
              w["gpost"]) + _ffn_consts(f)
    st_spec = _cur_seq(t, (GLA_HEADS, GLA_DK, GLA_DV))
    body = functools.partial(_gla_body, t)
    carried = [] if state is None else [state]
    return pl.pallas_call(
        _no_carry_in(body) if state is None else body,
        grid=(t.n_steps,),
        in_specs=[_cur_rows(t, D_MODEL)] + [st_spec] * len(carried) + [_const_spec(a) for a in consts],
        out_specs=[_prev_rows(t, D_MODEL), st_spec],
        out_shape=[jax.ShapeDtypeStruct((n_rows, D_MODEL), F32),
                   jax.ShapeDtypeStruct((t.n_batch * t.nseg, GLA_HEADS, GLA_DK, GLA_DV), F32)],
        scratch_shapes=[pltpu.VMEM((2, t.rows, D_MODEL), F32),
                        pltpu.VMEM((t.rows, D_FF), BF16),
                        pltpu.VMEM((t.nseg, GLA_HEADS, GLA_DK, GLA_DV), F32),
                        pltpu.VMEM((t.rows, D_V), BF16)],
        compiler_params=_params(),
        name="gla_layer_%d" % t.seg_len,
    )(x, *carried, *[_operand(a) for a in consts])


def _row(g):
    return g.reshape(1, -1)


def _trunk(t, emit_v, x, hist, state, we, wo, wf0, wf1):
    x, hist_out, *v_out = _even_call(t, emit_v, x, hist, we, wf0)
    x, state_out = _gla_call(t, x, state, wo, wf1)
    return x, hist_out, (v_out[0] if emit_v else None), state_out


def kernel(x_prompt, x_sample, state_pool, state_gla, pre_mix_g, post_mix_g, pre_ffn_g, post_ffn_g, w_in_even,
           w_pool, pool_scale, ln_v_g, ln_v_b, w_s, b_s, w_out_even, w_in_odd, w_gate2, b_gate, g_onorm,
           w_out_odd, w_ffn_gate, w_ffn_up, w_ffn_down):
    B, SEQ, _ = x_prompt.shape
    DB, DSEQ, _ = x_sample.shape

    we = dict(gpre=_row(pre_mix_g[0]), win=w_in_even[0].astype(BF16), wpool=w_pool[0].astype(BF16),
              pscale=_row(pool_scale[0]), lng=_row(ln_v_g[0]), lnb=_row(ln_v_b[0]), ws=w_s[0],
              bst=jnp.transpose(b_s[0]), wout=w_out_even[0].astype(BF16), gpost=_row(post_mix_g[0]))
    wo = dict(gpre=_row(pre_mix_g[1]), win=w_in_odd[0].astype(BF16), wg2=w_gate2[0].astype(BF16), bg=_row(b_gate[0]),
              gon=_row(g_onorm[0]), wout=w_out_odd[0].astype(BF16), gpost=_row(post_mix_g[1]))

    wg_all, wu_all, wd_all = w_ffn_gate.astype(BF16), w_ffn_up.astype(BF16), w_ffn_down.astype(BF16)

    def ffn_weights(l):
        return dict(gpre=_row(pre_ffn_g[l]), wg=LayerOf(wg_all, l), wu=LayerOf(wu_all, l), wd=LayerOf(wd_all, l),
                    gpost=_row(post_ffn_g[l]))

    wf0, wf1 = ffn_weights(0), ffn_weights(1)

    tp = Tiling(nseg=1, seg_len=PROMPT_TILE_ROWS, start=0, n_batch=B, n_tiles=SEQ // PROMPT_TILE_ROWS)
    yp, hp, _, sp = _trunk(tp, False, x_prompt.reshape(B * SEQ, D_MODEL), None, None, we, wo, wf0, wf1)

    ts = Tiling(nseg=DB, seg_len=DSEQ, start=PAST_LEN, n_batch=1, n_tiles=1)
    hist_s = jnp.pad(state_pool[0], ((0, 0), (HIST_ROWS - POOL_HIST, 0), (0, 0)))
    ys, hs, vs, ss = _trunk(ts, True, x_sample.reshape(DB * DSEQ, D_MODEL), hist_s, state_gla[0], we, wo, wf0, wf1)

    drop = HIST_ROWS - POOL_HIST
    return (yp.reshape(B, SEQ, D_MODEL), ys.reshape(DB, DSEQ, D_MODEL),
            hp[None, :, drop:], hs[None, :, drop:], vs.reshape(1, DB, DSEQ, D_SG), sp[None], ss[None])
```

```python
import functools
from typing import NamedTuple

import jax
import jax.numpy as jnp
from jax import lax
from jax.experimental import pallas as pl
from jax.experimental.pallas import tpu as pltpu

D_MODEL = 1024
PAST_LEN = 1024
CHUNK = 64
POOL_WINDOWS = (2, 4, 8, 16)
D_POOL = 512
POOL_GROUP = 128
POOL_HIST = 15
HIST_ROWS = 16
SG_CHUNK = 128
D_SG = 512
SG_HEADS = 4
SG_HEAD = 128
D_IN_EVEN = D_POOL + 2 * D_SG
GLA_HEADS = 4
GLA_DK = 128
GLA_DV = 256
GLA_GATE_RANK = 16
GLA_TAU = 16.0
D_QK = 512
D_V = 1024
OFF_Q, OFF_K, OFF_V, OFF_R, OFF_GLR = 0, D_QK, 2 * D_QK, 2 * D_QK + D_V, 2 * D_QK + 2 * D_V
D_FF = 2816
EPS = 1e-6
EVEN_BLOCK_ROWS = 256
GLA_BLOCK_ROWS = 256
EVEN_FFN_BLOCK_ROWS = 512
GLA_FFN_BLOCK_ROWS = 256
PROMPT_TILE_ROWS = 512
MXU_COLS = 256

F32 = jnp.float32
BF16 = jnp.bfloat16
V7X_VMEM_LIMIT_BYTES = 56 * 1024 * 1024


class Tiling(NamedTuple):
    nseg: int
    seg_len: int
    start: int
    n_batch: int
    n_tiles: int

    @property
    def rows(self):
        return self.nseg * self.seg_len

    @property
    def n_row_tiles(self):
        return self.n_batch * self.n_tiles

    @property
    def single_step(self):
        return self.n_row_tiles == 1

    @property
    def n_steps(self):
        return 1 if self.single_step else self.n_row_tiles + 1


def _rms(x, g):
    ms = jnp.mean(x * x, axis=-1, keepdims=True)
    return x * lax.rsqrt(ms + EPS) * g


def _gelu(x):
    c = 0.7978845608028654
    return x * (0.5 * (1.0 + jnp.tanh(c * (x + 0.044715 * (x * x * x)))))


def _silu(x):
    return x / (1.0 + jnp.exp(-x))


def _dot(a, b):
    return jnp.dot(a, b, preferred_element_type=F32)


def _interleave(a, b):
    if len(a) < len(b):
        a, b = b, a
    out, nxt = [], 0
    for n, f in enumerate(a):
        out.append(f)
        while nxt < len(b) and (nxt + 1) * len(a) <= (n + 1) * len(b):
            out.append(b[nxt])
            nxt += 1
    return out + b[nxt:]


def _mixer_schedule(blocks, project_steps, mix_steps, out_steps):
    n = len(blocks)
    states = [None] * n
    states[0], first = project_steps(blocks[0])
    segments = [first]
    for i in range(n):
        free = []
        if i + 1 < n:
            states[i + 1], free = project_steps(blocks[i + 1])
        if i > 0:
            free = _interleave(free, out_steps(blocks[i - 1], states[i - 1]))
        segments.append(_interleave(mix_steps(blocks[i], states[i]), free))
    return segments + [out_steps(blocks[n - 1], states[n - 1])]


def _merge_filler(segments, filler):
    weights = [1.0] * (len(segments) - 2) + [2.0, 0.0]
    if len(segments) == 3:
        weights = [1.0, 2.0, 0.0]
    total = sum(weights)
    order, used, acc = [], 0, 0.0
    for seg, wgt in zip(segments, weights):
        acc += wgt
        upto = len(filler) if acc >= total else int(round(len(filler) * acc / total))
        order += _interleave(seg, filler[used:upto])
        used = upto
    return order


def _out_steps(y_in, wout_ref, gpost_ref, x_ref, x1, w, rows):
    ys = []

    def unit(n):
        ys.append(_dot(y_in[rows, :], wout_ref[:, n * MXU_COLS:(n + 1) * MXU_COLS]))

    def finish():
        x1[w, rows, :] = x_ref[rows, :] + _rms(jnp.concatenate(ys, axis=-1), gpost_ref[...])

    return [functools.partial(unit, n) for n in range(D_MODEL // MXU_COLS)] + [finish]


def _ffn_steps(t, block_rows, x1, r, fw, xo_ref, act):
    gpre_ref, wg_ref, wu_ref, wd_ref, gpost_ref = fw
    nb = min(block_rows, t.rows)
    blocks = [slice(rb, rb + nb) for rb in range(0, t.rows, nb)]
    hs, ys = {}, {}

    def norm_in(i):
        hs[i] = _rms(x1[r, blocks[i], :], gpre_ref[...]).astype(BF16)

    def up(i, j):
        cols = slice(j * MXU_COLS, (j + 1) * MXU_COLS)
        act[blocks[i], cols] = (_silu(_dot(hs[i], wg_ref[:, cols])) * _dot(hs[i], wu_ref[:, cols])).astype(BF16)

    def down(i, n):
        ys.setdefault(i, []).append(_dot(act[blocks[i], :], wd_ref[:, n * MXU_COLS:(n + 1) * MXU_COLS]))

    def finish(i):
        xo_ref[blocks[i], :] = x1[r, blocks[i], :] + _rms(jnp.concatenate(ys[i], axis=-1), gpost_ref[...])

    ups = [[functools.partial(norm_in, i)] + [functools.partial(up, i, j) for j in range(D_FF // MXU_COLS)]
           for i in range(len(blocks))]
    downs = [[functools.partial(down, i, n) for n in range(D_MODEL // MXU_COLS)] + [functools.partial(finish, i)]
             for i in range(len(blocks))]
    return sum(ups, []) + sum(downs, [])


def _even_body(t, emit_v, x_ref, hist_ref, gpre_ref, win_ref, wpool_ref, pscale_ref, lng_ref, lnb_ref, ws_ref,
               bst_ref, wout_ref, gpost_ref, fgpre_ref, wg_ref, wu_ref, wd_ref, fgpost_ref, xo_ref, histo_ref,
               *rest):
    if emit_v:
        vo_ref, x1, act, abuf, ycat, wmask, wpool2 = rest
    else:
        x1, act, abuf, ycat, wmask, wpool2 = rest
    g = pl.program_id(0)
    last = t.n_row_tiles - 1
    tpos = jnp.minimum(g, last) % t.n_tiles
    L = t.seg_len
    nb = min(EVEN_BLOCK_ROWS, t.rows)
    c = min(L, SG_CHUNK)
    pair_heads = c == SG_CHUNK

    @pl.when(g == 0)
    def _():
        x1[1] = jnp.zeros(x1.shape[1:], F32)
        tril = (lax.broadcasted_iota(jnp.int32, (c, c), 0) >= lax.broadcasted_iota(jnp.int32, (c, c), 1))
        for hd in range(SG_HEADS):
            wmask[hd] = jnp.where(tril, ws_ref[hd, :c, :c], 0.0).astype(BF16)
        wpool2[...] = jnp.zeros(wpool2.shape, BF16)
        for gi in range(len(POOL_WINDOWS)):
            d = slice((gi % 2) * POOL_GROUP, (gi % 2 + 1) * POOL_GROUP)
            wpool2[gi // 2, d, d] = wpool_ref[gi]

    @pl.when(tpos == 0)
    def _():
        abuf[:, 0:HIST_ROWS, :] = jnp.zeros((t.nseg, HIST_ROWS, D_POOL), F32) if hist_ref is None else hist_ref[...]

    biases = [bst_ref[:c, hd:hd + 1] for hd in range(SG_HEADS)]
    iota_c = lax.broadcasted_iota(jnp.int32, (c, 1), 0)

    def pieces(rb):
        return [(p0 // L, p0 % L, slice(p0, p0 + c)) for p0 in range(rb, rb + nb, c)]

    def project_steps(rb):
        st = {"z": {}}
        rows = slice(rb, rb + nb)

        def unit(n):
            if n == 0:
                st["h"] = _rms(x_ref[rows, :], gpre_ref[...]).astype(BF16)
            st["z"][n] = _dot(st["h"], win_ref[:, n * MXU_COLS:(n + 1) * MXU_COLS])

        def pool_rows():
            st["a"] = a = jnp.concatenate([st["z"][0], st["z"][1]], axis=-1)
            for seg, off, pr in pieces(rb):
                abuf[seg, HIST_ROWS + off:HIST_ROWS + off + c, :] = a[pr.start - rb:pr.stop - rb]

        def gate_u():
            st["u"] = _gelu(jnp.concatenate([st["z"][2], st["z"][3]], axis=-1))

        def gate_v():
            gv = _gelu(jnp.concatenate([st["z"][4], st["z"][5]], axis=-1))
            mu = jnp.mean(gv, axis=-1, keepdims=True)
            dv = gv - mu
            var = jnp.mean(dv * dv, axis=-1, keepdims=True)
            vv = dv * lax.rsqrt(var + EPS) * lng_ref[...] + lnb_ref[...]
            st["vv"] = vv.astype(BF16)
            if emit_v:
                vo_ref[rows, :] = vv

        u = [functools.partial(unit, n) for n in range(D_IN_EVEN // MXU_COLS)]
        return st, [u[0], u[1], pool_rows, u[2], u[3], gate_u, u[4], u[5], gate_v]

    def mix_steps(rb, st):
        def window_mean(seg, off, pr, gi):
            win = POOL_WINDOWS[gi]
            cols = slice(gi * POOL_GROUP, (gi + 1) * POOL_GROUP)
            base = HIST_ROWS + off
            a = st["a"][pr.start - rb:pr.stop - rb, cols]
            acc = a
            for kk in range(1, win):
                acc = acc + abuf[seg, base - kk:base - kk + c, cols]
            cnt = jnp.minimum(win, t.start + tpos * L + off + iota_c + 1).astype(F32)
            return (acc / cnt - a).astype(BF16)

        def pool(seg, off, pr, p):
            cols = slice(2 * p * POOL_GROUP, (2 * p + 2) * POOL_GROUP)
            pooled = jnp.concatenate([window_mean(seg, off, pr, 2 * p), window_mean(seg, off, pr, 2 * p + 1)], axis=1)
            ya = _dot(pooled, wpool2[p]) * pscale_ref[:, cols]
            ycat[pr, cols] = ya.astype(BF16)

        def gate(pr, hd):
            cols = slice(hd * SG_HEAD, (hd + 1) * SG_HEAD)
            br = slice(pr.start - rb, pr.stop - rb)
            mixed = _dot(wmask[hd], st["vv"][br, cols]) + biases[hd]
            yb = st["u"][br, cols] * mixed
            ycat[pr, D_POOL + hd * SG_HEAD:D_POOL + (hd + 1) * SG_HEAD] = yb.astype(BF16)

        def gate_pair(pr, p):
            cols = slice(2 * p * SG_HEAD, (2 * p + 2) * SG_HEAD)
            br = slice(pr.start - rb, pr.stop - rb)
            va = st["vv"][br, 2 * p * SG_HEAD:(2 * p + 1) * SG_HEAD]
            vb = st["vv"][br, (2 * p + 1) * SG_HEAD:(2 * p + 2) * SG_HEAD]
            zero = jnp.zeros_like(va)
            vals = jnp.concatenate([jnp.concatenate([va, zero], axis=1), jnp.concatenate([zero, vb], axis=1)], axis=0)
            wpair = jnp.concatenate([wmask[2 * p], wmask[2 * p + 1]], axis=1)
            bias = jnp.concatenate([jnp.broadcast_to(biases[2 * p], (c, SG_HEAD)),
                                    jnp.broadcast_to(biases[2 * p + 1], (c, SG_HEAD))], axis=1)
            yb = st["u"][br, cols] * (_dot(wpair, vals) + bias)
            ycat[pr, D_POOL + cols.start:D_POOL + cols.stop] = yb.astype(BF16)

        steps = []
        for seg, off, pr in pieces(rb):
            steps += [functools.partial(pool, seg, off, pr, p) for p in range(len(POOL_WINDOWS) // 2)]
            if pair_heads:
                steps += [functools.partial(gate_pair, pr, p) for p in range(SG_HEADS // 2)]
            else:
                steps += [functools.partial(gate, pr, hd) for hd in range(SG_HEADS)]
        return steps

    def step(w, r):
        def out_steps(rb, st):
            return _out_steps(ycat, wout_ref, gpost_ref, x_ref, x1, w, slice(rb, rb + nb))

        mixer = _mixer_schedule(list(range(0, t.rows, nb)), project_steps, mix_steps, out_steps)
        ffn = _ffn_steps(t, EVEN_FFN_BLOCK_ROWS, x1, r, (fgpre_ref, wg_ref, wu_ref, wd_ref, fgpost_ref), xo_ref, act)
        for f in (sum(mixer, []) + ffn if t.single_step else _merge_filler(mixer, ffn)):
            f()
        abuf[:, 0:HIST_ROWS, :] = abuf[:, L:L + HIST_ROWS, :]

    if t.single_step:
        step(0, 0)
    else:
        step(g % 2, 1 - g % 2)

    @pl.when(g <= last)
    def _():
        histo_ref[...] = abuf[:, 0:HIST_ROWS, :]


def _gla_body(t, x_ref, st_ref, gpre_ref, win_ref, wg2_ref, bg_ref, gon_ref,
              wout_ref, gpost_ref, fgpre_ref, wg_ref, wu_ref, wd_ref, fgpost_ref, xo_ref, sto_ref,
              x1, act, S, og):
    g = pl.program_id(0)
    last = t.n_row_tiles - 1
    tpos = jnp.minimum(g, last) % t.n_tiles
    L = t.seg_len
    nb = min(GLA_BLOCK_ROWS, t.rows)
    c = min(CHUNK, L)
    cpb = nb // c

    @pl.when(g == 0)
    def _():
        x1[1] = jnp.zeros(x1.shape[1:], F32)

    @pl.when(tpos == 0)
    def _():
        S[...] = jnp.zeros(S.shape, F32) if st_ref is None else st_ref[...]

    ri = lax.broadcasted_iota(jnp.int32, (nb, nb), 0)
    ci = lax.broadcasted_iota(jnp.int32, (nb, nb), 1)
    causal = ((ri // c) == (ci // c)) & (ri >= ci)
    tri = jnp.where(causal, 1.0, 0.0).astype(BF16)

    def project_steps(rb):
        st = {}
        rows = slice(rb, rb + nb)

        def low_rank():
            st["h"] = _rms(x_ref[rows, :], gpre_ref[...]).astype(BF16)
            st["glr"] = _dot(st["h"], win_ref[:, OFF_GLR:OFF_GLR + GLA_GATE_RANK])

        def log_decay():
            gate = _dot(st["glr"].astype(BF16), wg2_ref[...]) + bg_ref[...]
            la = (jnp.minimum(gate, 0.0) - jnp.log(1.0 + jnp.exp(-jnp.abs(gate)))) * (1.0 / GLA_TAU)
            la_hi = la.astype(BF16)
            rem = la - la_hi.astype(F32)
            la_mid = rem.astype(BF16)
            st["la"] = (la_hi, la_mid, (rem - la_mid.astype(F32)).astype(BF16))

        def cum_decay():
            la_hi, la_mid, la_lo = st["la"]
            st["b"] = b = _dot(tri, la_hi) + _dot(tri, la_mid) + _dot(tri, la_lo)
            st["bT"] = [b[:, hd * GLA_DK:(hd + 1) * GLA_DK].T for hd in range(GLA_HEADS)]

        def heads_of(n):
            return [(2 * n + e, slice(e * GLA_DK, (e + 1) * GLA_DK)) for e in range(MXU_COLS // GLA_DK)]

        def query(n):
            cols = slice(n * MXU_COLS, (n + 1) * MXU_COLS)
            z = _dot(st["h"], win_ref[:, OFF_Q + cols.start:OFF_Q + cols.stop]) * (GLA_DK ** -0.5)
            st["qd2", n] = qd2 = (z * jnp.exp(st["b"][:, cols])).astype(BF16)
            for hd, half in heads_of(n):
                st["qd", hd] = qd2[:, half]

        def key(n):
            z = _dot(st["h"], win_ref[:, OFF_K + n * MXU_COLS:OFF_K + (n + 1) * MXU_COLS])
            for hd, half in heads_of(n):
                bT = st["bT"][hd]
                kT = z[:, half].T
                st["kiT", hd] = (kT * jnp.exp(-bT)).astype(BF16)
                ke, dec = [], []
                for j in range(cpb):
                    bl = bT[:, (j + 1) * c - 1:(j + 1) * c]
                    in_chunk = (lax.broadcasted_iota(jnp.int32, (GLA_DK, nb), 1) // c) == j
                    ke.append(jnp.where(in_chunk, kT * jnp.exp(bl - bT), 0.0).astype(BF16))
                    dec.append(jnp.exp(bl))
                st["ke", hd], st["dec", hd] = ke, dec

        def unit(name, off, n, post):
            def f():
                st[name, n] = post(_dot(st["h"], win_ref[:, off + n * MXU_COLS:off + (n + 1) * MXU_COLS]))
            return f

        ident = lambda z: z
        to_bf16 = lambda z: z.astype(BF16)
        vals = [unit("v", OFF_V, n, to_bf16) for n in range(D_V // MXU_COLS)]
        outg = [unit("r", OFF_R, n, ident) for n in range(D_V // MXU_COLS)]
        return st, ([low_rank, vals[0], log_decay, vals[1], cum_decay, vals[2]]
                    + [functools.partial(query, 0), functools.partial(key, 0), functools.partial(query, 1),
                       functools.partial(key, 1), vals[3]] + outg)

    def mix_steps(rb, st):
        res = {}

        def scores(n):
            ha, hb = 2 * n, 2 * n + 1
            zero = jnp.zeros((GLA_DK, nb), BF16)
            keys = jnp.concatenate([jnp.concatenate([st["kiT", ha], zero], axis=1),
                                    jnp.concatenate([zero, st["kiT", hb]], axis=1)], axis=0)
            raw = _dot(st["qd2", n], keys)
            res["araw", ha], res["araw", hb] = raw[:, :nb], raw[:, nb:]

        def values(hd):
            att = jnp.where(causal, res["araw", hd], 0.0).astype(BF16)
            res[hd] = _dot(jnp.concatenate([att] + st["ke", hd], axis=0), st["v", hd])

        def recur(j, hd):
            vc = slice(hd * GLA_DV, (hd + 1) * GLA_DV)
            seg = (rb + j * c) // L
            cr = slice(j * c, (j + 1) * c)
            Sh = S[seg, hd]
            o = _dot(st["qd", hd][cr], Sh.astype(BF16)) + res[hd][cr]
            S[seg, hd] = st["dec", hd][j] * Sh + res[hd][nb + j * GLA_DK:nb + (j + 1) * GLA_DK]
            on = o * lax.rsqrt(jnp.mean(o * o, axis=-1, keepdims=True) + EPS) * gon_ref[:, vc]
            og[rb + j * c:rb + (j + 1) * c, vc] = (on * _silu(st["r", hd][cr])).astype(BF16)

        heads = range(GLA_HEADS)
        return ([functools.partial(scores, n) for n in range(GLA_HEADS // 2)]
                + [functools.partial(values, hd) for hd in heads]
                + [functools.partial(recur, j, hd) for j in range(cpb) for hd in heads])

    def step(w, r):
        def out_steps(rb, st):
            return _out_steps(og, wout_ref, gpost_ref, x_ref, x1, w, slice(rb, rb + nb))

        mixer = _mixer_schedule(list(range(0, t.rows, nb)), project_steps, mix_steps, out_steps)
        ffn = _ffn_steps(t, GLA_FFN_BLOCK_ROWS, x1, r, (fgpre_ref, wg_ref, wu_ref, wd_ref, fgpost_ref), xo_ref, act)
        for f in (sum(mixer, []) + ffn if t.single_step else _merge_filler(mixer, ffn)):
            f()

    if t.single_step:
        step(0, 0)
    else:
        step(g % 2, 1 - g % 2)

    @pl.when(g <= last)
    def _():
        sto_ref[...] = S[...]


class LayerOf(NamedTuple):
    stack: jax.Array
    layer: int


def _const_spec(a):
    if isinstance(a, LayerOf):
        shape = a.stack.shape[1:]
        return pl.BlockSpec((None,) + shape, lambda g: (a.layer,) + (0,) * len(shape), pipeline_mode=pl.Buffered(1))
    nd = a.ndim
    return pl.BlockSpec(a.shape, lambda g: (0,) * nd, pipeline_mode=pl.Buffered(1))


def _operand(a):
    return a.stack if isinstance(a, LayerOf) else a


def _no_carry_in(body):
    def wrapped(x_ref, *refs):
        return body(x_ref, None, *refs)
    return wrapped


def _cur_rows(t, width):
    last = t.n_row_tiles - 1
    return pl.BlockSpec((t.rows, width), lambda g: (jnp.minimum(g, last), 0))


def _prev_rows(t, width):
    return pl.BlockSpec((t.rows, width), lambda g: (jnp.maximum(g - 1, 0), 0))


def _cur_seq(t, shape):
    nd = len(shape)
    last = t.n_row_tiles - 1
    return pl.BlockSpec((t.nseg,) + shape, lambda g: (jnp.minimum(g, last) // t.n_tiles,) + (0,) * nd)


def _params():
    return pltpu.CompilerParams(dimension_semantics=("arbitrary",), vmem_limit_bytes=V7X_VMEM_LIMIT_BYTES)


def _ffn_consts(f):
    return (f["gpre"], f["wg"], f["wu"], f["wd"], f["gpost"])


def _even_call(t, emit_v, x, hist, w, f):
    n_rows = x.shape[0]
    consts = (w["gpre"], w["win"], w["wpool"], w["pscale"], w["lng"], w["lnb"], w["ws"], w["bst"], w["wout"],
              w["gpost"]) + _ffn_consts(f)
    c = min(t.seg_len, SG_CHUNK)
    hist_spec = _cur_seq(t, (HIST_ROWS, D_POOL))
    out_shape = [jax.ShapeDtypeStruct((n_rows, D_MODEL), F32),
                 jax.ShapeDtypeStruct((t.n_batch * t.nseg, HIST_ROWS, D_POOL), F32)]
    out_specs = [_prev_rows(t, D_MODEL), hist_spec]
    if emit_v:
        out_shape.append(jax.ShapeDtypeStruct((n_rows, D_SG), F32))
        out_specs.append(_cur_rows(t, D_SG))
    body = functools.partial(_even_body, t, emit_v)
    carried = [] if hist is None else [hist]
    return pl.pallas_call(
        _no_carry_in(body) if hist is None else body,
        grid=(t.n_steps,),
        in_specs=[_cur_rows(t, D_MODEL)] + [hist_spec] * len(carried) + [_const_spec(a) for a in consts],
        out_specs=out_specs,
        out_shape=out_shape,
        scratch_shapes=[pltpu.VMEM((2, t.rows, D_MODEL), F32),
                        pltpu.VMEM((t.rows, D_FF), BF16),
                        pltpu.VMEM((t.nseg, HIST_ROWS + t.seg_len, D_POOL), F32),
                        pltpu.VMEM((t.rows, D_POOL + D_SG), BF16),
                        pltpu.VMEM((SG_HEADS, c, c), BF16),
                        pltpu.VMEM((len(POOL_WINDOWS) // 2, MXU_COLS, MXU_COLS), BF16)],
        compiler_params=_params(),
        name="even_layer_%d" % t.seg_len,
    )(x, *carried, *[_operand(a) for a in consts])


def _gla_call(t, x, state, w, f):
    n_rows = x.shape[0]
    consts = (w["gpre"], w["win"], w["wg2"], w["bg"], w["gon"], w["wout"],
              w["gpost"]) + _ffn_consts(f)
    st_spec = _cur_seq(t, (GLA_HEADS, GLA_DK, GLA_DV))
    body = functools.partial(_gla_body, t)
    carried = [] if state is None else [state]
    return pl.pallas_call(
        _no_carry_in(body) if state is None else body,
        grid=(t.n_steps,),
        in_specs=[_cur_rows(t, D_MODEL)] + [st_spec] * len(carried) + [_const_spec(a) for a in consts],
        out_specs=[_prev_rows(t, D_MODEL), st_spec],
        out_shape=[jax.ShapeDtypeStruct((n_rows, D_MODEL), F32),
                   jax.ShapeDtypeStruct((t.n_batch * t.nseg, GLA_HEADS, GLA_DK, GLA_DV), F32)],
        scratch_shapes=[pltpu.VMEM((2, t.rows, D_MODEL), F32),
                        pltpu.VMEM((t.rows, D_FF), BF16),
                        pltpu.VMEM((t.nseg, GLA_HEADS, GLA_DK, GLA_DV), F32),
                        pltpu.VMEM((t.rows, D_V), BF16)],
        compiler_params=_params(),
        name="gla_layer_%d" % t.seg_len,
    )(x, *carried, *[_operand(a) for a in consts])


def _row(g):
    return g.reshape(1, -1)


def _trunk(t, emit_v, x, hist, state, we, wo, wf0, wf1):
    x, hist_out, *v_out = _even_call(t, emit_v, x, hist, we, wf0)
    x, state_out = _gla_call(t, x, state, wo, wf1)
    return x, hist_out, (v_out[0] if emit_v else None), state_out


def kernel(x_prompt, x_sample, state_pool, state_gla, pre_mix_g, post_mix_g, pre_ffn_g, post_ffn_g, w_in_even,
           w_pool, pool_scale, ln_v_g, ln_v_b, w_s, b_s, w_out_even, w_in_odd, w_gate2, b_gate, g_onorm,
           w_out_odd, w_ffn_gate, w_ffn_up, w_ffn_down):
    B, SEQ, _ = x_prompt.shape
    DB, DSEQ, _ = x_sample.shape

    we = dict(gpre=_row(pre_mix_g[0]), win=w_in_even[0].astype(BF16), wpool=w_pool[0].astype(BF16),
              pscale=_row(pool_scale[0]), lng=_row(ln_v_g[0]), lnb=_row(ln_v_b[0]), ws=w_s[0],
              bst=jnp.transpose(b_s[0]), wout=w_out_even[0].astype(BF16), gpost=_row(post_mix_g[0]))
    wo = dict(gpre=_row(pre_mix_g[1]), win=w_in_odd[0].astype(BF16), wg2=w_gate2[0].astype(BF16), bg=_row(b_gate[0]),
              gon=_row(g_onorm[0]), wout=w_out_odd[0].astype(BF16), gpost=_row(post_mix_g[1]))

    wg_all, wu_all, wd_all = w_ffn_gate.astype(BF16), w_ffn_up.astype(BF16), w_ffn_down.astype(BF16)

    def ffn_weights(l):
        return dict(gpre=_row(pre_ffn_g[l]), wg=LayerOf(wg_all, l), wu=LayerOf(wu_all, l), wd=LayerOf(wd_all, l),
                    gpost=_row(post_ffn_g[l]))

    wf0, wf1 = ffn_weights(0), ffn_weights(1)

    tp = Tiling(nseg=1, seg_len=PROMPT_TILE_ROWS, start=0, n_batch=B, n_tiles=SEQ // PROMPT_TILE_ROWS)
    yp, hp, _, sp = _trunk(tp, False, x_prompt.reshape(B * SEQ, D_MODEL), None, None, we, wo, wf0, wf1)

    ts = Tiling(nseg=DB, seg_len=DSEQ, start=PAST_LEN, n_batch=1, n_tiles=1)
    hist_s = jnp.pad(state_pool[0], ((0, 0), (HIST_ROWS - POOL_HIST, 0), (0, 0)))
    ys, hs, vs, ss = _trunk(ts, True, x_sample.reshape(DB * DSEQ, D_MODEL), hist_s, state_gla[0], we, wo, wf0, wf1)

    drop = HIST_ROWS - POOL_HIST
    return (yp.reshape(B, SEQ, D_MODEL), ys.reshape(DB, DSEQ, D_MODEL),
            hp[None, :, drop:], hs[None, :, drop:], vs.reshape(1, DB, DSEQ, D_SG), sp[None], ss[None])
```

```python
import functools
from typing import NamedTuple

import jax
import jax.numpy as jnp
from jax import lax
from jax.experimental import pallas as pl
from jax.experimental.pallas import tpu as pltpu

D_MODEL = 1024
PAST_LEN = 1024
CHUNK = 64
POOL_WINDOWS = (2, 4, 8, 16)
D_POOL = 512
POOL_GROUP = 128
POOL_HIST = 15
HIST_ROWS = 16
SG_CHUNK = 128
D_SG = 512
SG_HEADS = 4
SG_HEAD = 128
D_IN_EVEN = D_POOL + 2 * D_SG
GLA_HEADS = 4
GLA_DK = 128
GLA_DV = 256
GLA_GATE_RANK = 16
GLA_TAU = 16.0
D_QK = 512
D_V = 1024
OFF_Q, OFF_K, OFF_V, OFF_R, OFF_GLR = 0, D_QK, 2 * D_QK, 2 * D_QK + D_V, 2 * D_QK + 2 * D_V
D_FF = 2816
EPS = 1e-6
EVEN_BLOCK_ROWS = 512
GLA_BLOCK_ROWS = 256
EVEN_FFN_BLOCK_ROWS = 512
GLA_FFN_BLOCK_ROWS = 256
PROMPT_TILE_ROWS = 512
MXU_COLS = 256

F32 = jnp.float32
BF16 = jnp.bfloat16
V7X_VMEM_LIMIT_BYTES = 56 * 1024 * 1024


class Tiling(NamedTuple):
    nseg: int
    seg_len: int
    start: int
    n_batch: int
    n_tiles: int

    @property
    def rows(self):
        return self.nseg * self.seg_len

    @property
    def n_row_tiles(self):
        return self.n_batch * self.n_tiles

    @property
    def single_step(self):
        return self.n_row_tiles == 1

    @property
    def n_steps(self):
        return 1 if self.single_step else self.n_row_tiles + 1


def _rms(x, g):
    ms = jnp.mean(x * x, axis=-1, keepdims=True)
    return x * lax.rsqrt(ms + EPS) * g


def _gelu(x):
    c = 0.7978845608028654
    return x * (0.5 * (1.0 + jnp.tanh(c * (x + 0.044715 * (x * x * x)))))


def _silu(x):
    return x / (1.0 + jnp.exp(-x))


def _dot(a, b):
    return jnp.dot(a, b, preferred_element_type=F32)


def _interleave(a, b):
    if len(a) < len(b):
        a, b = b, a
    out, nxt = [], 0
    for n, f in enumerate(a):
        out.append(f)
        while nxt < len(b) and (nxt + 1) * len(a) <= (n + 1) * len(b):
            out.append(b[nxt])
            nxt += 1
    return out + b[nxt:]


def _mixer_schedule(blocks, project_steps, mix_steps, out_steps):
    n = len(blocks)
    states = [None] * n
    states[0], first = project_steps(blocks[0])
    segments = [first]
    for i in range(n):
        free = []
        if i + 1 < n:
            states[i + 1], free = project_steps(blocks[i + 1])
        if i > 0:
            free = _interleave(free, out_steps(blocks[i - 1], states[i - 1]))
        segments.append(_interleave(mix_steps(blocks[i], states[i]), free))
    return segments + [out_steps(blocks[n - 1], states[n - 1])]


def _merge_filler(segments, filler):
    weights = [1.0] * (len(segments) - 2) + [2.0, 0.0]
    if len(segments) == 3:
        weights = [1.0, 2.0, 0.0]
    total = sum(weights)
    order, used, acc = [], 0, 0.0
    for seg, wgt in zip(segments, weights):
        acc += wgt
        upto = len(filler) if acc >= total else int(round(len(filler) * acc / total))
        order += _interleave(seg, filler[used:upto])
        used = upto
    return order


def _out_steps(y_in, wout_ref, gpost_ref, x_ref, x1, w, rows):
    ys = []

    def unit(n):
        ys.append(_dot(y_in[rows, :], wout_ref[:, n * MXU_COLS:(n + 1) * MXU_COLS]))

    def finish():
        x1[w, rows, :] = x_ref[rows, :] + _rms(jnp.concatenate(ys, axis=-1), gpost_ref[...])

    return [functools.partial(unit, n) for n in range(D_MODEL // MXU_COLS)] + [finish]


def _ffn_steps(t, block_rows, x1, r, fw, xo_ref, act):
    gpre_ref, wg_ref, wu_ref, wd_ref, gpost_ref = fw
    nb = min(block_rows, t.rows)
    blocks = [slice(rb, rb + nb) for rb in range(0, t.rows, nb)]
    hs, ys = {}, {}

    def norm_in(i):
        hs[i] = _rms(x1[r, blocks[i], :], gpre_ref[...]).astype(BF16)

    def up(i, j):
        cols = slice(j * MXU_COLS, (j + 1) * MXU_COLS)
        act[blocks[i], cols] = (_silu(_dot(hs[i], wg_ref[:, cols])) * _dot(hs[i], wu_ref[:, cols])).astype(BF16)

    def down(i, n):
        ys.setdefault(i, []).append(_dot(act[blocks[i], :], wd_ref[:, n * MXU_COLS:(n + 1) * MXU_COLS]))

    def finish(i):
        xo_ref[blocks[i], :] = x1[r, blocks[i], :] + _rms(jnp.concatenate(ys[i], axis=-1), gpost_ref[...])

    ups = [[functools.partial(norm_in, i)] + [functools.partial(up, i, j) for j in range(D_FF // MXU_COLS)]
           for i in range(len(blocks))]
    downs = [[functools.partial(down, i, n) for n in range(D_MODEL // MXU_COLS)] + [functools.partial(finish, i)]
             for i in range(len(blocks))]
    return sum(ups, []) + sum(downs, [])


def _even_body(t, emit_v, x_ref, hist_ref, gpre_ref, win_ref, wpool_ref, pscale_ref, lng_ref, lnb_ref, ws_ref,
               bst_ref, wout_ref, gpost_ref, fgpre_ref, wg_ref, wu_ref, wd_ref, fgpost_ref, xo_ref, histo_ref,
               *rest):
    if emit_v:
        vo_ref, x1, act, abuf, ycat, wmask, wpool2 = rest
    else:
        x1, act, abuf, ycat, wmask, wpool2 = rest
    g = pl.program_id(0)
    last = t.n_row_tiles - 1
    tpos = jnp.minimum(g, last) % t.n_tiles
    L = t.seg_len
    nb = min(EVEN_BLOCK_ROWS, t.rows)
    c = min(L, SG_CHUNK)
    pair_heads = c == SG_CHUNK

    @pl.when(g == 0)
    def _():
        x1[1] = jnp.zeros(x1.shape[1:], F32)
        tril = (lax.broadcasted_iota(jnp.int32, (c, c), 0) >= lax.broadcasted_iota(jnp.int32, (c, c), 1))
        for hd in range(SG_HEADS):
            wmask[hd] = jnp.where(tril, ws_ref[hd, :c, :c], 0.0).astype(BF16)
        wpool2[...] = jnp.zeros(wpool2.shape, BF16)
        for gi in range(len(POOL_WINDOWS)):
            d = slice((gi % 2) * POOL_GROUP, (gi % 2 + 1) * POOL_GROUP)
            wpool2[gi // 2, d, d] = wpool_ref[gi]

    @pl.when(tpos == 0)
    def _():
        abuf[:, 0:HIST_ROWS, :] = jnp.zeros((t.nseg, HIST_ROWS, D_POOL), F32) if hist_ref is None else hist_ref[...]

    biases = [bst_ref[:c, hd:hd + 1] for hd in range(SG_HEADS)]
    iota_c = lax.broadcasted_iota(jnp.int32, (c, 1), 0)

    def pieces(rb):
        return [(p0 // L, p0 % L, slice(p0, p0 + c)) for p0 in range(rb, rb + nb, c)]

    def project_steps(rb):
        st = {"z": {}}
        rows = slice(rb, rb + nb)

        def unit(n):
            if n == 0:
                st["h"] = _rms(x_ref[rows, :], gpre_ref[...]).astype(BF16)
            st["z"][n] = _dot(st["h"], win_ref[:, n * MXU_COLS:(n + 1) * MXU_COLS])

        def pool_rows():
            st["a"] = a = jnp.concatenate([st["z"][0], st["z"][1]], axis=-1)
            for seg, off, pr in pieces(rb):
                abuf[seg, HIST_ROWS + off:HIST_ROWS + off + c, :] = a[pr.start - rb:pr.stop - rb]

        def gate_u():
            st["u"] = _gelu(jnp.concatenate([st["z"][2], st["z"][3]], axis=-1))

        def gate_v():
            gv = _gelu(jnp.concatenate([st["z"][4], st["z"][5]], axis=-1))
            mu = jnp.mean(gv, axis=-1, keepdims=True)
            dv = gv - mu
            var = jnp.mean(dv * dv, axis=-1, keepdims=True)
            vv = dv * lax.rsqrt(var + EPS) * lng_ref[...] + lnb_ref[...]
            st["vv"] = vv.astype(BF16)
            if emit_v:
                vo_ref[rows, :] = vv

        u = [functools.partial(unit, n) for n in range(D_IN_EVEN // MXU_COLS)]
        return st, [u[0], u[1], pool_rows, u[2], u[3], gate_u, u[4], u[5], gate_v]

    def mix_steps(rb, st):
        def window_mean(seg, off, pr, gi):
            win = POOL_WINDOWS[gi]
            cols = slice(gi * POOL_GROUP, (gi + 1) * POOL_GROUP)
            base = HIST_ROWS + off
            a = st["a"][pr.start - rb:pr.stop - rb, cols]
            acc = a
            for kk in range(1, win):
                acc = acc + abuf[seg, base - kk:base - kk + c, cols]
            cnt = jnp.minimum(win, t.start + tpos * L + off + iota_c + 1).astype(F32)
            return (acc / cnt - a).astype(BF16)

        def pool(seg, off, pr, p):
            cols = slice(2 * p * POOL_GROUP, (2 * p + 2) * POOL_GROUP)
            pooled = jnp.concatenate([window_mean(seg, off, pr, 2 * p), window_mean(seg, off, pr, 2 * p + 1)], axis=1)
            ya = _dot(pooled, wpool2[p]) * pscale_ref[:, cols]
            ycat[pr, cols] = ya.astype(BF16)

        def gate(pr, hd):
            cols = slice(hd * SG_HEAD, (hd + 1) * SG_HEAD)
            br = slice(pr.start - rb, pr.stop - rb)
            mixed = _dot(wmask[hd], st["vv"][br, cols]) + biases[hd]
            yb = st["u"][br, cols] * mixed
            ycat[pr, D_POOL + hd * SG_HEAD:D_POOL + (hd + 1) * SG_HEAD] = yb.astype(BF16)

        def gate_pair(pr, p):
            cols = slice(2 * p * SG_HEAD, (2 * p + 2) * SG_HEAD)
            br = slice(pr.start - rb, pr.stop - rb)
            va = st["vv"][br, 2 * p * SG_HEAD:(2 * p + 1) * SG_HEAD]
            vb = st["vv"][br, (2 * p + 1) * SG_HEAD:(2 * p + 2) * SG_HEAD]
            zero = jnp.zeros_like(va)
            vals = jnp.concatenate([jnp.concatenate([va, zero], axis=1), jnp.concatenate([zero, vb], axis=1)], axis=0)
            wpair = jnp.concatenate([wmask[2 * p], wmask[2 * p + 1]], axis=1)
            bias = jnp.concatenate([jnp.broadcast_to(biases[2 * p], (c, SG_HEAD)),
                                    jnp.broadcast_to(biases[2 * p + 1], (c, SG_HEAD))], axis=1)
            yb = st["u"][br, cols] * (_dot(wpair, vals) + bias)
            ycat[pr, D_POOL + cols.start:D_POOL + cols.stop] = yb.astype(BF16)

        steps = []
        for seg, off, pr in pieces(rb):
            steps += [functools.partial(pool, seg, off, pr, p) for p in range(len(POOL_WINDOWS) // 2)]
            if pair_heads:
                steps += [functools.partial(gate_pair, pr, p) for p in range(SG_HEADS // 2)]
            else:
                steps += [functools.partial(gate, pr, hd) for hd in range(SG_HEADS)]
        return steps

    def step(w, r):
        def out_steps(rb, st):
            return _out_steps(ycat, wout_ref, gpost_ref, x_ref, x1, w, slice(rb, rb + nb))

        mixer = _mixer_schedule(list(range(0, t.rows, nb)), project_steps, mix_steps, out_steps)
        ffn = _ffn_steps(t, EVEN_FFN_BLOCK_ROWS, x1, r, (fgpre_ref, wg_ref, wu_ref, wd_ref, fgpost_ref), xo_ref, act)
        for f in (sum(mixer, []) + ffn if t.single_step else _merge_filler(mixer, ffn)):
            f()
        abuf[:, 0:HIST_ROWS, :] = abuf[:, L:L + HIST_ROWS, :]

    if t.single_step:
        step(0, 0)
    else:
        step(g % 2, 1 - g % 2)

    @pl.when(g <= last)
    def _():
        histo_ref[...] = abuf[:, 0:HIST_ROWS, :]


def _gla_body(t, x_ref, st_ref, gpre_ref, win_ref, wg2_ref, bg_ref, gon_ref,
              wout_ref, gpost_ref, fgpre_ref, wg_ref, wu_ref, wd_ref, fgpost_ref, xo_ref, sto_ref,
              x1, act, S, og):
    g = pl.program_id(0)
    last = t.n_row_tiles - 1
    tpos = jnp.minimum(g, last) % t.n_tiles
    L = t.seg_len
    nb = min(GLA_BLOCK_ROWS, t.rows)
    c = min(CHUNK, L)
    cpb = nb // c

    @pl.when(g == 0)
    def _():
        x1[1] = jnp.zeros(x1.shape[1:], F32)

    @pl.when(tpos == 0)
    def _():
        S[...] = jnp.zeros(S.shape, F32) if st_ref is None else st_ref[...]

    ri = lax.broadcasted_iota(jnp.int32, (nb, nb), 0)
    ci = lax.broadcasted_iota(jnp.int32, (nb, nb), 1)
    causal = ((ri // c) == (ci // c)) & (ri >= ci)
    tri = jnp.where(causal, 1.0, 0.0).astype(BF16)

    def project_steps(rb):
        st = {}
        rows = slice(rb, rb + nb)

        def low_rank():
            st["h"] = _rms(x_ref[rows, :], gpre_ref[...]).astype(BF16)
            st["glr"] = _dot(st["h"], win_ref[:, OFF_GLR:OFF_GLR + GLA_GATE_RANK])

        def log_decay():
            gate = _dot(st["glr"].astype(BF16), wg2_ref[...]) + bg_ref[...]
            la = (jnp.minimum(gate, 0.0) - jnp.log(1.0 + jnp.exp(-jnp.abs(gate)))) * (1.0 / GLA_TAU)
            la_hi = la.astype(BF16)
            rem = la - la_hi.astype(F32)
            la_mid = rem.astype(BF16)
            st["la"] = (la_hi, la_mid, (rem - la_mid.astype(F32)).astype(BF16))

        def cum_decay():
            la_hi, la_mid, la_lo = st["la"]
            st["b"] = b = _dot(tri, la_hi) + _dot(tri, la_mid) + _dot(tri, la_lo)
            st["bT"] = [b[:, hd * GLA_DK:(hd + 1) * GLA_DK].T for hd in range(GLA_HEADS)]

        def heads_of(n):
            return [(2 * n + e, slice(e * GLA_DK, (e + 1) * GLA_DK)) for e in range(MXU_COLS // GLA_DK)]

        def query(n):
            cols = slice(n * MXU_COLS, (n + 1) * MXU_COLS)
            z = _dot(st["h"], win_ref[:, OFF_Q + cols.start:OFF_Q + cols.stop]) * (GLA_DK ** -0.5)
            st["qd2", n] = qd2 = (z * jnp.exp(st["b"][:, cols])).astype(BF16)
            for hd, half in heads_of(n):
                st["qd", hd] = qd2[:, half]

        def key(n):
            z = _dot(st["h"], win_ref[:, OFF_K + n * MXU_COLS:OFF_K + (n + 1) * MXU_COLS])
            for hd, half in heads_of(n):
                bT = st["bT"][hd]
                kT = z[:, half].T
                st["kiT", hd] = (kT * jnp.exp(-bT)).astype(BF16)
                ke, dec = [], []
                for j in range(cpb):
                    bl = bT[:, (j + 1) * c - 1:(j + 1) * c]
                    in_chunk = (lax.broadcasted_iota(jnp.int32, (GLA_DK, nb), 1) // c) == j
                    ke.append(jnp.where(in_chunk, kT * jnp.exp(bl - bT), 0.0).astype(BF16))
                    dec.append(jnp.exp(bl))
                st["ke", hd], st["dec", hd] = ke, dec

        def unit(name, off, n, post):
            def f():
                st[name, n] = post(_dot(st["h"], win_ref[:, off + n * MXU_COLS:off + (n + 1) * MXU_COLS]))
            return f

        ident = lambda z: z
        to_bf16 = lambda z: z.astype(BF16)
        vals = [unit("v", OFF_V, n, to_bf16) for n in range(D_V // MXU_COLS)]
        outg = [unit("r", OFF_R, n, ident) for n in range(D_V // MXU_COLS)]
        return st, ([low_rank, vals[0], log_decay, vals[1], cum_decay, vals[2]]
                    + [functools.partial(query, 0), functools.partial(key, 0), functools.partial(query, 1),
                       functools.partial(key, 1), vals[3]] + outg)

    def mix_steps(rb, st):
        res = {}

        def scores(n):
            ha, hb = 2 * n, 2 * n + 1
            zero = jnp.zeros((GLA_DK, nb), BF16)
            keys = jnp.concatenate([jnp.concatenate([st["kiT", ha], zero], axis=1),
                                    jnp.concatenate([zero, st["kiT", hb]], axis=1)], axis=0)
            raw = _dot(st["qd2", n], keys)
            res["araw", ha], res["araw", hb] = raw[:, :nb], raw[:, nb:]

        def values(hd):
            att = jnp.where(causal, res["araw", hd], 0.0).astype(BF16)
            res[hd] = _dot(jnp.concatenate([att] + st["ke", hd], axis=0), st["v", hd])

        def recur(j, hd):
            vc = slice(hd * GLA_DV, (hd + 1) * GLA_DV)
            seg = (rb + j * c) // L
            cr = slice(j * c, (j + 1) * c)
            Sh = S[seg, hd]
            o = _dot(st["qd", hd][cr], Sh.astype(BF16)) + res[hd][cr]
            S[seg, hd] = st["dec", hd][j] * Sh + res[hd][nb + j * GLA_DK:nb + (j + 1) * GLA_DK]
            on = o * lax.rsqrt(jnp.mean(o * o, axis=-1, keepdims=True) + EPS) * gon_ref[:, vc]
            og[rb + j * c:rb + (j + 1) * c, vc] = (on * _silu(st["r", hd][cr])).astype(BF16)

        heads = range(GLA_HEADS)
        return ([functools.partial(scores, n) for n in range(GLA_HEADS // 2)]
                + [functools.partial(values, hd) for hd in heads]
                + [functools.partial(recur, j, hd) for j in range(cpb) for hd in heads])

    def step(w, r):
        def out_steps(rb, st):
            return _out_steps(og, wout_ref, gpost_ref, x_ref, x1, w, slice(rb, rb + nb))

        mixer = _mixer_schedule(list(range(0, t.rows, nb)), project_steps, mix_steps, out_steps)
        ffn = _ffn_steps(t, GLA_FFN_BLOCK_ROWS, x1, r, (fgpre_ref, wg_ref, wu_ref, wd_ref, fgpost_ref), xo_ref, act)
        for f in (sum(mixer, []) + ffn if t.single_step else _merge_filler(mixer, ffn)):
            f()

    if t.single_step:
        step(0, 0)
    else:
        step(g % 2, 1 - g % 2)

    @pl.when(g <= last)
    def _():
        sto_ref[...] = S[...]


class LayerOf(NamedTuple):
    stack: jax.Array
    layer: int


def _const_spec(a):
    if isinstance(a, LayerOf):
        shape = a.stack.shape[1:]
        return pl.BlockSpec((None,) + shape, lambda g: (a.layer,) + (0,) * len(shape), pipeline_mode=pl.Buffered(1))
    nd = a.ndim
    return pl.BlockSpec(a.shape, lambda g: (0,) * nd, pipeline_mode=pl.Buffered(1))


def _operand(a):
    return a.stack if isinstance(a, LayerOf) else a


def _no_carry_in(body):
    def wrapped(x_ref, *refs):
        return body(x_ref, None, *refs)
    return wrapped


def _cur_rows(t, width):
    last = t.n_row_tiles - 1
    return pl.BlockSpec((t.rows, width), lambda g: (jnp.minimum(g, last), 0))


def _prev_rows(t, width):
    return pl.BlockSpec((t.rows, width), lambda g: (jnp.maximum(g - 1, 0), 0))


def _cur_seq(t, shape):
    nd = len(shape)
    last = t.n_row_tiles - 1
    return pl.BlockSpec((t.nseg,) + shape, lambda g: (jnp.minimum(g, last) // t.n_tiles,) + (0,) * nd)


def _params():
    return pltpu.CompilerParams(dimension_semantics=("arbitrary",), vmem_limit_bytes=V7X_VMEM_LIMIT_BYTES)


def _ffn_consts(f):
    return (f["gpre"], f["wg"], f["wu"], f["wd"], f["gpost"])


def _even_call(t, emit_v, x, hist, w, f):
    n_rows = x.shape[0]
    consts = (w["gpre"], w["win"], w["wpool"], w["pscale"], w["lng"], w["lnb"], w["ws"], w["bst"], w["wout"],
              w["gpost"]) + _ffn_consts(f)
    c = min(t.seg_len, SG_CHUNK)
    hist_spec = _cur_seq(t, (HIST_ROWS, D_POOL))
    out_shape = [jax.ShapeDtypeStruct((n_rows, D_MODEL), F32),
                 jax.ShapeDtypeStruct((t.n_batch * t.nseg, HIST_ROWS, D_POOL), F32)]
    out_specs = [_prev_rows(t, D_MODEL), hist_spec]
    if emit_v:
        out_shape.append(jax.ShapeDtypeStruct((n_rows, D_SG), F32))
        out_specs.append(_cur_rows(t, D_SG))
    body = functools.partial(_even_body, t, emit_v)
    carried = [] if hist is None else [hist]
    return pl.pallas_call(
        _no_carry_in(body) if hist is None else body,
        grid=(t.n_steps,),
        in_specs=[_cur_rows(t, D_MODEL)] + [hist_spec] * len(carried) + [_const_spec(a) for a in consts],
        out_specs=out_specs,
        out_shape=out_shape,
        scratch_shapes=[pltpu.VMEM((2, t.rows, D_MODEL), F32),
                        pltpu.VMEM((t.rows, D_FF), BF16),
                        pltpu.VMEM((t.nseg, HIST_ROWS + t.seg_len, D_POOL), F32),
                        pltpu.VMEM((t.rows, D_POOL + D_SG), BF16),
                        pltpu.VMEM((SG_HEADS, c, c), BF16),
                        pltpu.VMEM((len(POOL_WINDOWS) // 2, MXU_COLS, MXU_COLS), BF16)],
        compiler_params=_params(),
        name="even_layer_%d" % t.seg_len,
    )(x, *carried, *[_operand(a) for a in consts])


def _gla_call(t, x, state, w, f):
    n_rows = x.shape[0]
    consts = (w["gpre"], w["win"], w["wg2"], w["bg"], w["gon"], w["wout"],
              w["gpost"]) + _ffn_consts(f)
    st_spec = _cur_seq(t, (GLA_HEADS, GLA_DK, GLA_DV))
    body = functools.partial(_gla_body, t)
    carried = [] if state is None else [state]
    return pl.pallas_call(
        _no_carry_in(body) if state is None else body,
        grid=(t.n_steps,),
        in_specs=[_cur_rows(t, D_MODEL)] + [st_spec] * len(carried) + [_const_spec(a) for a in consts],
        out_specs=[_prev_rows(t, D_MODEL), st_spec],
        out_shape=[jax.ShapeDtypeStruct((n_rows, D_MODEL), F32),
                   jax.ShapeDtypeStruct((t.n_batch * t.nseg, GLA_HEADS, GLA_DK, GLA_DV), F32)],
        scratch_shapes=[pltpu.VMEM((2, t.rows, D_MODEL), F32),
                        pltpu.VMEM((t.rows, D_FF), BF16),
                        pltpu.VMEM((t.nseg, GLA_HEADS, GLA_DK, GLA_DV), F32),
                        pltpu.VMEM((t.rows, D_V), BF16)],
        compiler_params=_params(),
        name="gla_layer_%d" % t.seg_len,
    )(x, *carried, *[_operand(a) for a in consts])


def _row(g):
    return g.reshape(1, -1)


def _trunk(t, emit_v, x, hist, state, we, wo, wf0, wf1):
    x, hist_out, *v_out = _even_call(t, emit_v, x, hist, we, wf0)
    x, state_out = _gla_call(t, x, state, wo, wf1)
    return x, hist_out, (v_out[0] if emit_v else None), state_out


def kernel(x_prompt, x_sample, state_pool, state_gla, pre_mix_g, post_mix_g, pre_ffn_g, post_ffn_g, w_in_even,
           w_pool, pool_scale, ln_v_g, ln_v_b, w_s, b_s, w_out_even, w_in_odd, w_gate2, b_gate, g_onorm,
           w_out_odd, w_ffn_gate, w_ffn_up, w_ffn_down):
    B, SEQ, _ = x_prompt.shape
    DB, DSEQ, _ = x_sample.shape

    we = dict(gpre=_row(pre_mix_g[0]), win=w_in_even[0].astype(BF16), wpool=w_pool[0].astype(BF16),
              pscale=_row(pool_scale[0]), lng=_row(ln_v_g[0]), lnb=_row(ln_v_b[0]), ws=w_s[0],
              bst=jnp.transpose(b_s[0]), wout=w_out_even[0].astype(BF16), gpost=_row(post_mix_g[0]))
    wo = dict(gpre=_row(pre_mix_g[1]), win=w_in_odd[0].astype(BF16), wg2=w_gate2[0].astype(BF16), bg=_row(b_gate[0]),
              gon=_row(g_onorm[0]), wout=w_out_odd[0].astype(BF16), gpost=_row(post_mix_g[1]))

    wg_all, wu_all, wd_all = w_ffn_gate.astype(BF16), w_ffn_up.astype(BF16), w_ffn_down.astype(BF16)

    def ffn_weights(l):
        return dict(gpre=_row(pre_ffn_g[l]), wg=LayerOf(wg_all, l), wu=LayerOf(wu_all, l), wd=LayerOf(wd_all, l),
                    gpost=_row(post_ffn_g[l]))

    wf0, wf1 = ffn_weights(0), ffn_weights(1)

    tp = Tiling(nseg=1, seg_len=PROMPT_TILE_ROWS, start=0, n_batch=B, n_tiles=SEQ // PROMPT_TILE_ROWS)
    yp, hp, _, sp = _trunk(tp, False, x_prompt.reshape(B * SEQ, D_MODEL), None, None, we, wo, wf0, wf1)

    ts = Tiling(nseg=DB, seg_len=DSEQ, start=PAST_LEN, n_batch=1, n_tiles=1)
    hist_s = jnp.pad(state_pool[0], ((0, 0), (HIST_ROWS - POOL_HIST, 0), (0, 0)))
    ys, hs, vs, ss = _trunk(ts, True, x_sample.reshape(DB * DSEQ, D_MODEL), hist_s, state_gla[0], we, wo, wf0, wf1)

    drop = HIST_ROWS - POOL_HIST
    return (yp.reshape(B, SEQ, D_MODEL), ys.reshape(DB, DSEQ, D_MODEL),
            hp[None, :, drop:], hs[None, :, drop:], vs.reshape(1, DB, DSEQ, D_SG), sp[None], ss[None])
```
